```python
import jax, jax.numpy as jnp
from jax import lax
import numpy as np

D_MODEL = 1024
BATCH = 32
SEQ = 256
DEPTH = 4
DEC_BATCH = 4
DEC_SEQ = 4096
PAST_LEN = 512

GRID_W = 64
N_HEADS_A = 8
HEAD_K = 128
HEAD_V = 128
D_A = N_HEADS_A * HEAD_V
N_GROUPS_B = 4
GROUP_B = 128
D_B = N_GROUPS_B * GROUP_B
CHUNK = 32
EPS = 1e-6
SPLITS = (D_A, 2 * D_A, 3 * D_A, 4 * D_A, 5 * D_A, 5 * D_A + D_B, 5 * D_A + 2 * D_B,
          5 * D_A + 2 * D_B + D_MODEL)
D_IN = 5 * D_A + 2 * D_B + 2 * D_MODEL

kernel_name = "hgrn2_fnet_gated_diffusion_step"


def rms_norm(x, w):
    xf = x.astype(jnp.float32)
    y = xf * lax.rsqrt(jnp.mean(xf * xf, axis=-1, keepdims=True) + EPS)
    return (y * w.astype(jnp.float32)).astype(x.dtype)


def layer_lower_bounds(lb_raw):
    p = jax.nn.softmax(lb_raw.astype(jnp.float32), axis=0)
    cs = jnp.cumsum(p, axis=0)
    return cs - cs[0:1]


def grid_pos_embed(length, d):
    rows = length // GRID_W
    r = jnp.repeat(jnp.arange(rows, dtype=jnp.float32), GRID_W)
    col = jnp.tile(jnp.arange(GRID_W, dtype=jnp.float32), rows)
    nf = d // 4
    freqs = 1.0 / (10000.0 ** (jnp.arange(nf, dtype=jnp.float32) / nf))

    def emb(p):
        a = p[:, None] * freqs[None, :]
        return jnp.concatenate([jnp.sin(a), jnp.cos(a)], axis=-1)

    return jnp.concatenate([emb(r), emb(col)], axis=-1)


def hgrn2_chunk_scan(q, log_f, k, v, s0):
    B, L, H, _ = q.shape
    n = L // CHUNK

    def to_chunks(t):
        return t.reshape(B, n, CHUNK, H, t.shape[-1]).transpose(1, 0, 3, 2, 4)

    qc, gc, kc, vc = to_chunks(q), to_chunks(log_f), to_chunks(k), to_chunks(v)
    mask = jnp.tril(jnp.ones((CHUNK, CHUNK), dtype=bool))[:, :, None]

    def step(S, inp):
        qi, gi, ki, vi = inp
        b = jnp.cumsum(gi, axis=-2)
        diff = b[..., :, None, :] - b[..., None, :, :]
        decay = jnp.exp(jnp.where(mask, diff, -jnp.inf))
        att = jnp.einsum('bhtc,bhsc,bhtsc->bhts', qi, ki, decay)
        o = (jnp.einsum('bhts,bhsv->bhtv', att, vi)
             + jnp.einsum('bhtc,bhcv->bhtv', qi * jnp.exp(b), S))
        b_last = b[..., -1:, :]
        S_new = (jnp.exp(b_last[..., 0, :])[..., None] * S
                 + jnp.einsum('bhsc,bhsv->bhcv', ki * jnp.exp(b_last - b), vi))
        return S_new, o

    S_fin, o = lax.scan(step, s0, (qc, gc, kc, vc))
    o = o.transpose(1, 0, 3, 2, 4).reshape(B, L, H, -1)
    return o, S_fin


def forget_gate(fpre, lb):
    lbh = lb.reshape(N_HEADS_A, HEAD_K)
    log_f = jnp.logaddexp(jnp.log(lbh), jnp.log1p(-lbh) + jax.nn.log_sigmoid(fpre))
    return log_f, 1.0 - jnp.exp(log_f)


def mixer(h, lb_f, lb_b, w_in, gnorm_w, w_pa, w_pb, w_o, s0_f, s0_b):
    B, L, _ = h.shape
    proj = h @ w_in
    q, ff, fb, vi, zA, u, zB, gA, gB = jnp.split(proj, SPLITS, axis=-1)

    def heads(t):
        return t.reshape(B, L, N_HEADS_A, -1).astype(jnp.float32)

    qh = jax.nn.silu(heads(q))
    vh = heads(vi)
    logf_f, k_f = forget_gate(heads(ff), lb_f)
    logf_b, k_b = forget_gate(heads(fb), lb_b)
    o_f, S_f = hgrn2_chunk_scan(qh, logf_f, k_f, vh, s0_f.astype(jnp.float32))
    o_b_rev, S_b = hgrn2_chunk_scan(qh[:, ::-1], logf_b[:, ::-1], k_b[:, ::-1], vh[:, ::-1],
                                    s0_b.astype(jnp.float32))
    o = o_f + o_b_rev[:, ::-1]
    o = o * lax.rsqrt(jnp.mean(o * o, axis=-1, keepdims=True) + EPS)
    o = o * gnorm_w.astype(jnp.float32).reshape(N_HEADS_A, HEAD_V)
    yA = (o.reshape(B, L, D_A) * jax.nn.silu(zA.astype(jnp.float32))).astype(h.dtype)

    ug = u.astype(jnp.float32).reshape(B, L, N_GROUPS_B, GROUP_B)
    yF = jnp.fft.fft2(ug, axes=(1, 3), norm='ortho').real.reshape(B, L, D_B)
    yB = (yF * jax.nn.silu(zB.astype(jnp.float32))).astype(h.dtype)

    merged = jax.nn.sigmoid(gA) * (yA @ w_pa) + jax.nn.sigmoid(gB) * (yB @ w_pb)
    return merged @ w_o, S_f, S_b


def setup_inputs(seed: int = 0) -> dict:
    key = jax.random.key(seed)
    ks = jax.random.split(key, 16)
    D = D_MODEL
    f32 = jnp.float32
    return {
        "x_prompt": jax.random.normal(ks[0], (BATCH, SEQ, D), f32),
        "x_sample": jax.random.normal(ks[1], (DEC_BATCH, DEC_SEQ, D), f32),
        "state_hgrn": 0.3 * jax.random.normal(ks[2], (DEC_BATCH, DEPTH, 2, N_HEADS_A, HEAD_K, HEAD_V), f32),
        "c": jax.random.normal(ks[3], (DEC_BATCH, D), f32),
        "c_ctx": jax.random.normal(ks[4], (D,), f32),
        "norm_w": 1.0 + 0.01 * jax.random.normal(ks[5], (DEPTH, D), f32),
        "w_ada": 0.5 * D ** -0.5 * jax.random.normal(ks[6], (DEPTH, D, 3 * D), f32),
        "b_ada": 0.02 * jax.random.normal(ks[7], (DEPTH, 3 * D), f32),
        "w_in": D ** -0.5 * jax.random.normal(ks[8], (DEPTH, D, D_IN), f32),
        "lb_raw": 0.5 * jax.random.normal(ks[9], (DEPTH, 2, D_A), f32),
        "gnorm_w": 1.0 + 0.01 * jax.random.normal(ks[10], (DEPTH, D_A), f32),
        "w_pa": D_A ** -0.5 * jax.random.normal(ks[11], (DEPTH, D_A, D), f32),
        "w_pb": D_B ** -0.5 * jax.random.normal(ks[12], (DEPTH, D_B, D), f32),
        "w_o": D ** -0.5 * jax.random.normal(ks[13], (DEPTH, D, D), f32),
        "final_norm_w": 1.0 + 0.01 * jax.random.normal(ks[14], (D,), f32),
    }


def reference(x_prompt, x_sample, state_hgrn, c, c_ctx, norm_w, w_ada, b_ada, w_in, lb_raw,
              gnorm_w, w_pa, w_pb, w_o, final_norm_w):
    D = D_MODEL
    lbs = layer_lower_bounds(lb_raw)
    xp = x_prompt
    xs = x_sample + grid_pos_embed(x_sample.shape[1], D).astype(x_sample.dtype)[None]
    Bp = xp.shape[0]
    zero_state = jnp.zeros((Bp, N_HEADS_A, HEAD_K, HEAD_V), jnp.float32)
    ctx_states = []
    for l in range(DEPTH):
        mod_p = jax.nn.silu(c_ctx) @ w_ada[l] + b_ada[l]
        sh_p, sc_p, gt_p = mod_p[:D], mod_p[D:2 * D], mod_p[2 * D:]
        hp = rms_norm(xp, norm_w[l]) * (1.0 + sc_p) + sh_p
        out_p, sf, sb = mixer(hp, lbs[l, 0], lbs[l, 1], w_in[l], gnorm_w[l], w_pa[l], w_pb[l],
                              w_o[l], zero_state, zero_state)
        xp = xp + gt_p * out_p
        ctx_states.append(jnp.stack([sf, sb], axis=1).astype(x_prompt.dtype))
        mod_s = jax.nn.silu(c) @ w_ada[l] + b_ada[l]
        sh_s, sc_s, gt_s = mod_s[:, None, :D], mod_s[:, None, D:2 * D], mod_s[:, None, 2 * D:]
        hs = rms_norm(xs, norm_w[l]) * (1.0 + sc_s) + sh_s
        out_s, _, _ = mixer(hs, lbs[l, 0], lbs[l, 1], w_in[l], gnorm_w[l], w_pa[l], w_pb[l],
                            w_o[l], state_hgrn[:, l, 0], state_hgrn[:, l, 1])
        xs = xs + gt_s * out_s
    y_prompt = rms_norm(xp, final_norm_w)
    y_sample = rms_norm(xs, final_norm_w)
    new_state = jnp.stack(ctx_states, axis=1)
    return (y_prompt, y_sample, new_state)
```

```python
import functools

import jax
import jax.numpy as jnp
import numpy as np
from jax import lax
from jax.experimental import pallas as pl
from jax.experimental.pallas import tpu as pltpu

D = 1024
DEPTH = 4
N_HEADS = 8
HEAD = 128
D_A = N_HEADS * HEAD
N_GROUPS = 4
GROUP = 128
D_B = N_GROUPS * GROUP
D_IN = 5 * D_A + 2 * D_B + 2 * D
GRID_W = 64
EPS = 1e-6

V7X_VMEM_LIMIT_BYTES = 56 * 1024 * 1024

SCAN_CHUNK = 64
EXP_CLAMP = 80.0
TM_PROJ = 256
TM_MERGE = 512
FFT_RADIX = 64
FFT_BLOCK = 8
MOD_ROWS = 8

F32 = jnp.float32
BF16 = jnp.bfloat16


def _cparams(n_axes):
    return pltpu.CompilerParams(
        dimension_semantics=("arbitrary",) * n_axes,
        vmem_limit_bytes=V7X_VMEM_LIMIT_BYTES,
    )


def _dot(a, b):
    return jnp.dot(a, b, preferred_element_type=F32)


def _dot_nt(a, b):
    return lax.dot_general(a, b, (((1,), (1,)), ((), ())), preferred_element_type=F32)


def _dot_tn(a, b):
    return lax.dot_general(a, b, (((0,), (0,)), ((), ())), preferred_element_type=F32)


def _sigmoid(x):
    u = jnp.exp(-jnp.abs(x))
    r = 1.0 / (1.0 + u)
    return jnp.where(x >= 0, r, u * r)


def _silu(x):
    return x * _sigmoid(x)


def _dft_cos_sin(n):
    k = np.arange(n, dtype=np.float64)
    ang = 2.0 * np.pi * np.outer(k, k) / n
    return np.cos(ang), np.sin(ang)


def _pos_embed_table(length, d):
    rows = length // GRID_W
    r = np.repeat(np.arange(rows, dtype=np.float64), GRID_W)
    col = np.tile(np.arange(GRID_W, dtype=np.float64), rows)
    nf = d // 4
    freqs = 1.0 / (10000.0 ** (np.arange(nf, dtype=np.float64) / nf))

    def emb(p):
        a = p[:, None] * freqs[None, :]
        return np.concatenate([np.sin(a), np.cos(a)], axis=-1)

    return np.concatenate([emb(r), emb(col)], axis=-1).astype(np.float32)


def _lb_kernel(raw_ref, out_ref):
    x = raw_ref[...]
    m = jnp.max(x, axis=0, keepdims=True)
    e = jnp.exp(x - m)
    p = e / jnp.sum(e, axis=0, keepdims=True)
    cs = p[0:1]
    first = cs
    out_ref[0:1, :] = cs - first
    for l in range(1, DEPTH):
        cs = cs + p[l:l + 1]
        out_ref[l:l + 1, :] = cs - first


def _lower_bounds(lb_raw):
    raw = lb_raw.reshape(DEPTH, 2 * D_A)
    out = pl.pallas_call(
        _lb_kernel,
        out_shape=jax.ShapeDtypeStruct((DEPTH, 2 * D_A), F32),
        name="lower_bounds",
    )(raw)
    return out.reshape(DEPTH * 2, 1, D_A)


def _mod_kernel(cond_ref, w_ref, b_ref, out_ref):
    a = _silu(cond_ref[...])
    out_ref[...] = jnp.dot(a, w_ref[...], preferred_element_type=F32,
                           precision=lax.Precision.HIGHEST) + b_ref[...]


def _modulation(cond, w_ada, b_ada):
    tn = 512
    out = pl.pallas_call(
        _mod_kernel,
        grid=(DEPTH, 3 * D // tn),
        in_specs=[
            pl.BlockSpec((MOD_ROWS, D), lambda l, j: (0, 0)),
            pl.BlockSpec((None, D, tn), lambda l, j: (l, 0, j)),
            pl.BlockSpec((None, 1, tn), lambda l, j: (l, 0, j)),
        ],
        out_specs=pl.BlockSpec((None, MOD_ROWS, tn), lambda l, j: (l, 0, j)),
        out_shape=jax.ShapeDtypeStruct((DEPTH, MOD_ROWS, 3 * D), F32),
        compiler_params=_cparams(2),
        name="modulation",
    )(cond, w_ada, b_ada.reshape(DEPTH, 1, 3 * D))
    return out.reshape(DEPTH * MOD_ROWS, 1, 3 * D)


_Q0, _FF0, _V0, _ZA0, _U0, _ZB0, _GA0, _GB0 = (
    0, D_A, 3 * D_A, 4 * D_A, 5 * D_A, 5 * D_A + D_B, 5 * D_A + 2 * D_B, 5 * D_A + 2 * D_B + D)
PA_Q, PA_V, PA_Z, PA_GA, PA_GB = 0, 1, 2, 3, 4
PA_WIDTH = 5 * D


def _proj_kernel(has_pe, *refs):
    if has_pe:
        x_ref, pe_ref, mod_ref, nw_ref, w_ref, cc_ref, sc_ref, pf_ref, pa_ref, zr_ref, zi_ref, zb_ref = refs
        x = x_ref[...] + pe_ref[...]
    else:
        x_ref, mod_ref, nw_ref, w_ref, cc_ref, sc_ref, pf_ref, pa_ref, zr_ref, zi_ref, zb_ref = refs
        x = x_ref[...]
    ms = jnp.mean(x * x, axis=-1, keepdims=True)
    y = x * lax.rsqrt(ms + EPS) * nw_ref[...]
    shift = mod_ref[:, 0:D]
    scale = mod_ref[:, D:2 * D]
    h = (y * (1.0 + scale) + shift).astype(BF16)

    def sec(c0, width):
        return _dot(h, w_ref[:, c0:c0 + width])

    pf_ref[...] = sec(_FF0, 2 * D_A)
    pa_ref[:, PA_Q * D:(PA_Q + 1) * D] = sec(_Q0, D_A).astype(BF16)
    pa_ref[:, PA_V * D:(PA_V + 1) * D] = sec(_V0, D_A).astype(BF16)
    pa_ref[:, PA_Z * D:(PA_Z + 1) * D] = sec(_ZA0, D_A).astype(BF16)
    pa_ref[:, PA_GA * D:(PA_GA + 1) * D] = sec(_GA0, D).astype(BF16)
    pa_ref[:, PA_GB * D:(PA_GB + 1) * D] = sec(_GB0, D).astype(BF16)
    zb_ref[...] = sec(_ZB0, D_B).astype(BF16)
    u = sec(_U0, D_B).astype(BF16)
    cc = cc_ref[...].astype(BF16)
    sc = sc_ref[...].astype(BF16)
    for g in range(N_GROUPS):
        ug = u[:, g * GROUP:(g + 1) * GROUP]
        zr_ref[:, g * GROUP:(g + 1) * GROUP] = _dot(ug, cc).astype(BF16)
        zi_ref[:, g * GROUP:(g + 1) * GROUP] = (-_dot(ug, sc)).astype(BF16)


def _proj(x, pe, mod, norm_w3, w_in_bf, cc, sc, layer, seq_len, latent):
    tokens = x.shape[0]
    tm = TM_PROJ
    has_pe = pe is not None
    per_seq = seq_len // tm
    if latent:
        mod_idx = lambda i: (layer * MOD_ROWS + 1 + i // per_seq, 0, 0)
    else:
        mod_idx = lambda i: (layer * MOD_ROWS, 0, 0)
    in_specs = [pl.BlockSpec((tm, D), lambda i: (i, 0))]
    args = [x]
    if has_pe:
        in_specs.append(pl.BlockSpec((tm, D), lambda i: (i % per_seq, 0)))
        args.append(pe)
    in_specs += [
        pl.BlockSpec((None, 1, 3 * D), mod_idx),
        pl.BlockSpec((None, 1, D), lambda i: (layer, 0, 0)),
        pl.BlockSpec((None, D, D_IN), lambda i: (layer, 0, 0), pipeline_mode=pl.Buffered(1)),
        pl.BlockSpec((GROUP, GROUP), lambda i: (0, 0)),
        pl.BlockSpec((GROUP, GROUP), lambda i: (0, 0)),
    ]
    args += [mod, norm_w3, w_in_bf, cc, sc]
    out_shape = (
        jax.ShapeDtypeStruct((tokens, 2 * D_A), F32),
        jax.ShapeDtypeStruct((tokens, PA_WIDTH), BF16),
        jax.ShapeDtypeStruct((tokens, D_B), BF16),
        jax.ShapeDtypeStruct((tokens, D_B), BF16),
        jax.ShapeDtypeStruct((tokens, D_B), BF16),
    )
    out_specs = (
        pl.BlockSpec((tm, 2 * D_A), lambda i: (i, 0)),
        pl.BlockSpec((tm, PA_WIDTH), lambda i: (i, 0)),
        pl.BlockSpec((tm, D_B), lambda i: (i, 0)),
        pl.BlockSpec((tm, D_B), lambda i: (i, 0)),
        pl.BlockSpec((tm, D_B), lambda i: (i, 0)),
    )
    return pl.pallas_call(
        functools.partial(_proj_kernel, has_pe),
        grid=(tokens // tm,),
        in_specs=in_specs,
        out_specs=out_specs,
        out_shape=out_shape,
        compiler_params=_cparams(1),
        name="proj",
    )(*args)


def _forget_gate(x, lb):
    u = jnp.exp(-jnp.abs(x))
    r = 1.0 / (1.0 + u)
    sig = jnp.where(x >= 0, r, u * r)
    nsig = jnp.where(x >= 0, u * r, r)
    f = lb + (1.0 - lb) * sig
    log_f = jnp.where(f < 1e-30, jnp.minimum(x, 0.0), jnp.log(jnp.maximum(f, 1e-30)))
    return log_f, (1.0 - lb) * nsig


def _scan_kernel(seq_len, has_init, emit_state, *refs):
    refs = list(refs)
    ff_ref, fb_ref, q_ref, v_ref, z_ref, lbf_ref, lbb_ref, gn_ref = refs[:8]
    pos = 8
    init_ref = None
    if has_init:
        init_ref = refs[pos]
        pos += 1
    y_ref = refs[pos]
    pos += 1
    st_ref = None
    if emit_state:
        st_ref = refs[pos]
        pos += 1
    of_ref, ob_ref, sf_ref, sb_ref = refs[pos:pos + 4]

    C = SCAN_CHUNK
    half = C // 2
    n_chunks = seq_len // C

    row = lax.broadcasted_iota(jnp.int32, (C, C), 0)
    col = lax.broadcasted_iota(jnp.int32, (C, C), 1)
    causal = col <= row
    anti = col >= row
    tri_f = jnp.where(causal, 1.0, 0.0).astype(BF16)
    tri_b = jnp.where(anti, 1.0, 0.0).astype(BF16)

    if has_init:
        sf_ref[...] = init_ref[0].T
        sb_ref[...] = init_ref[1].T
    else:
        sf_ref[...] = jnp.zeros((HEAD, HEAD), F32)
        sb_ref[...] = jnp.zeros((HEAD, HEAD), F32)

    lbf = lbf_ref[...]
    lbb = lbb_ref[...]

    def chunk(row0, f_ref, lb, tri, mask, mid_row, last_row, s_ref, o_ref):
        x = f_ref[pl.ds(row0, C), :]
        q = _silu(q_ref[pl.ds(row0, C), :].astype(F32))
        v = v_ref[pl.ds(row0, C), :]
        g, kk = _forget_gate(x, lb)
        g_hi = g.astype(BF16)
        g_lo = (g - g_hi.astype(F32)).astype(BF16)
        b = _dot(tri, g_hi) + _dot(tri, g_lo)
        r = b[mid_row:mid_row + 1, :]
        b_last = b[last_row:last_row + 1, :]
        e = b - r
        qt = q * jnp.exp(jnp.minimum(e, EXP_CLAMP))
        kt = kk * jnp.exp(jnp.minimum(-e, EXP_CLAMP))
        q_in = (qt * jnp.exp(r)).astype(BF16)
        k_out = (kt * jnp.exp(b_last - r)).astype(BF16)
        att = _dot_nt(qt.astype(BF16), kt.astype(BF16))
        att = jnp.where(mask, att, 0.0).astype(BF16)
        s_t = s_ref[...]
        o_ref[pl.ds(row0, C), :] = _dot(att, v) + _dot_nt(q_in, s_t.astype(BF16))
        s_ref[...] = s_t * jnp.exp(b_last) + _dot_tn(v, k_out)

    def body(i, carry):
        rf = pl.multiple_of(i * C, C)
        rb = pl.multiple_of((n_chunks - 1 - i) * C, C)
        chunk(rf, ff_ref, lbf, tri_f, causal, half - 1, C - 1, sf_ref, of_ref)
        chunk(rb, fb_ref, lbb, tri_b, anti, half, 0, sb_ref, ob_ref)
        return carry

    lax.fori_loop(0, n_chunks, body, 0)

    gn = gn_ref[...]
    rows = 256

    def finish(j, carry):
        r0 = pl.multiple_of(j * rows, rows)
        o = of_ref[pl.ds(r0, rows), :] + ob_ref[pl.ds(r0, rows), :]
        ms = jnp.mean(o * o, axis=-1, keepdims=True)
        o = o * lax.rsqrt(ms + EPS) * gn
        zg = _silu(z_ref[pl.ds(r0, rows), :].astype(F32))
        y_ref[pl.ds(r0, rows), :] = (o * zg).astype(BF16)
        return carry

    lax.fori_loop(0, seq_len // rows, finish, 0)

    if emit_state:
        st_ref[0] = sf_ref[...].T
        st_ref[1] = sb_ref[...].T


def _scan(pf, pa, lbs, gnorm3, init_state, layer, n_seq, seq_len, emit_state):
    tokens = n_seq * seq_len
    has_init = init_state is not None
    blk = lambda c0: pl.BlockSpec((seq_len, HEAD), lambda b, h, c0=c0: (b, c0 + h))
    in_specs = [
        blk(0), blk(N_HEADS),
        blk(PA_Q * N_HEADS), blk(PA_V * N_HEADS), blk(PA_Z * N_HEADS),
        pl.BlockSpec((None, 1, HEAD), lambda b, h: (2 * layer, 0, h)),
        pl.BlockSpec((None, 1, HEAD), lambda b, h: (2 * layer + 1, 0, h)),
        pl.BlockSpec((None, 1, HEAD), lambda b, h: (layer, 0, h)),
    ]
    args = [pf, pf, pa, pa, pa, lbs, lbs, gnorm3]
    if has_init:
        in_specs.append(pl.BlockSpec((None, None, 2, None, HEAD, HEAD),
                                     lambda b, h: (b, layer, 0, h, 0, 0)))
        args.append(init_state)
    out_shape = [jax.ShapeDtypeStruct((tokens, D_A), BF16)]
    out_specs = [pl.BlockSpec((seq_len, HEAD), lambda b, h: (b, h))]
    if emit_state:
        out_shape.append(jax.ShapeDtypeStruct((n_seq, 2, N_HEADS, HEAD, HEAD), F32))
        out_specs.append(pl.BlockSpec((None, 2, None, HEAD, HEAD), lambda b, h: (b, 0, h, 0, 0)))
    res = pl.pallas_call(
        functools.partial(_scan_kernel, seq_len, has_init, emit_state),
        grid=(n_seq, N_HEADS),
        in_specs=in_specs,
        out_specs=out_specs,
        out_shape=out_shape,
        scratch_shapes=[
            pltpu.VMEM((seq_len, HEAD), F32), pltpu.VMEM((seq_len, HEAD), F32),
            pltpu.VMEM((HEAD, HEAD), F32), pltpu.VMEM((HEAD, HEAD), F32),
        ],
        compiler_params=_cparams(2),
        name="scan",
    )(*args)
    return res if emit_state else (res[0], None)


def _fourier_dense_kernel(scale, zr_ref, zi_ref, zb_ref, c_ref, s_ref, y_ref):
    yf = (_dot(c_ref[...].astype(BF16), zr_ref[...])
          + _dot(s_ref[...].astype(BF16), zi_ref[...]))
    y_ref[...] = (yf * scale * _silu(zb_ref[...].astype(F32))).astype(BF16)


def _fourier_dense(zr, zi, zb, n_seq, seq_len):
    c, s = _dft_cos_sin(seq_len)
    scale = 1.0 / np.sqrt(seq_len * GROUP)
    blk = pl.BlockSpec((seq_len, D_B), lambda b: (b, 0))
    mat = pl.BlockSpec((seq_len, seq_len), lambda b: (0, 0))
    return pl.pallas_call(
        functools.partial(_fourier_dense_kernel, scale),
        grid=(n_seq,),
        in_specs=[blk, blk, blk, mat, mat],
        out_specs=blk,
        out_shape=jax.ShapeDtypeStruct((n_seq * seq_len, D_B), BF16),
        compiler_params=_cparams(1),
        name="fourier_dense",
    )(zr, zi, zb, jnp.asarray(c, F32), jnp.asarray(s, F32))


def _fourier_stage1_kernel(zr_ref, zi_ref, m_ref, tc_ref, ts_ref, br_ref, bi_ref):
    z = jnp.concatenate([zr_ref[...], zi_ref[...]], axis=0)
    a = _dot(m_ref[...].astype(BF16), z)
    ar = a[:FFT_RADIX]
    ai = a[FFT_RADIX:]
    tc = tc_ref[...]
    ts = ts_ref[...]
    br_ref[...] = (ar * tc + ai * ts).astype(BF16)
    bi_ref[...] = (ai * tc - ar * ts).astype(BF16)


def _fourier_stage2_kernel(scale, br_ref, bi_ref, zb_ref, cs_ref, y_ref):
    cs = cs_ref[...].astype(BF16)
    for j in range(FFT_BLOCK):
        bcat = jnp.concatenate([br_ref[j], bi_ref[j]], axis=0)
        yf = _dot(cs, bcat) * scale
        gate = _silu(zb_ref[:, j * D_B:(j + 1) * D_B].astype(F32))
        y_ref[:, j * D_B:(j + 1) * D_B] = (yf * gate).astype(BF16)


def _fourier_two_stage(zr, zi, zb, n_seq, seq_len):
    R = FFT_RADIX
    assert seq_len == R * R
    wide = R * D_B
    cols = FFT_BLOCK * D_B
    c, s = _dft_cos_sin(R)
    m1 = np.block([[c, s], [-s, c]])
    k1 = np.arange(R, dtype=np.float64)
    ang = 2.0 * np.pi * np.outer(k1, k1) / seq_len
    tc = np.repeat(np.cos(ang), D_B, axis=1).astype(np.float32)
    ts = np.repeat(np.sin(ang), D_B, axis=1).astype(np.float32)
    cs = np.concatenate([c, s], axis=1)
    scale = 1.0 / np.sqrt(seq_len * GROUP)

    zr3 = zr.reshape(n_seq, R, wide)
    zi3 = zi.reshape(n_seq, R, wide)
    seq_blk = pl.BlockSpec((None, R, cols), lambda b, j: (b, 0, j))
    br, bi = pl.pallas_call(
        _fourier_stage1_kernel,
        grid=(n_seq, R // FFT_BLOCK),
        in_specs=[seq_blk, seq_blk,
                  pl.BlockSpec((2 * R, 2 * R), lambda b, j: (0, 0)),
                  pl.BlockSpec((R, cols), lambda b, j: (0, j)),
                  pl.BlockSpec((R, cols), lambda b, j: (0, j))],
        out_specs=(seq_blk, seq_blk),
        out_shape=(jax.ShapeDtypeStruct((n_seq, R, wide), BF16),) * 2,
        compiler_params=_cparams(2),
        name="fourier_stage1",
    )(zr3, zi3, jnp.asarray(m1, F32), jnp.asarray(tc), jnp.asarray(ts))

    br4 = br.reshape(n_seq, R, R, D_B)
    bi4 = bi.reshape(n_seq, R, R, D_B)
    zb3 = zb.reshape(n_seq, R, wide)
    k_blk = pl.BlockSpec((None, FFT_BLOCK, R, D_B), lambda b, j: (b, j, 0, 0))
    y = pl.pallas_call(
        functools.partial(_fourier_stage2_kernel, scale),
        grid=(n_seq, R // FFT_BLOCK),
        in_specs=[k_blk, k_blk, seq_blk, pl.BlockSpec((R, 2 * R), lambda b, j: (0, 0))],
        out_specs=seq_blk,
        out_shape=jax.ShapeDtypeStruct((n_seq, R, wide), BF16),
        compiler_params=_cparams(2),
        name="fourier_stage2",
    )(br4, bi4, zb3, jnp.asarray(cs, F32))
    return y.reshape(n_seq * seq_len, D_B)


def _merge_kernel(has_pe, final, *refs):
    refs = list(refs)
    ya_ref, yb_ref, ga_ref, gb_ref, x_ref = refs[:5]
    pos = 5
    x = x_ref[...]
    if has_pe:
        x = x + refs[pos][...]
        pos += 1
    mod_ref, wpa_ref, wpb_ref, wo_ref = refs[pos:pos + 4]
    pos += 4
    if final:
        fw_ref = refs[pos]
        pos += 1
    out_ref = refs[pos]
    merged = (_sigmoid(ga_ref[...].astype(F32)) * _dot(ya_ref[...], wpa_ref[...])
              + _sigmoid(gb_ref[...].astype(F32)) * _dot(yb_ref[...], wpb_ref[...]))
    out = _dot(merged.astype(BF16), wo_ref[...])
    xn = x + mod_ref[:, 2 * D:3 * D] * out
    if final:
        ms = jnp.mean(xn * xn, axis=-1, keepdims=True)
        xn = xn * lax.rsqrt(ms + EPS) * fw_ref[...]
    out_ref[...] = xn


def _merge(ya, yb, pa, x, pe, mod, wpa, wpb, wo, final_w, layer, seq_len, latent):
    tokens = x.shape[0]
    tm = TM_MERGE
    has_pe = pe is not None
    final = final_w is not None
    row = lambda i: (i, 0)
    if latent:
        mod_idx = lambda i: (layer * MOD_ROWS + 1 + (i * tm) // seq_len, 0, 0)
    else:
        mod_idx = lambda i: (layer * MOD_ROWS, 0, 0)
    in_specs = [
        pl.BlockSpec((tm, D_A), row),
        pl.BlockSpec((tm, D_B), row),
        pl.BlockSpec((tm, D), lambda i: (i, PA_GA)),
        pl.BlockSpec((tm, D), lambda i: (i, PA_GB)),
        pl.BlockSpec((tm, D), row),
    ]
    args = [ya, yb, pa, pa, x]
    if has_pe:
        per_seq = seq_len // tm
        in_specs.append(pl.BlockSpec((tm, D), lambda i: (i % per_seq, 0)))
        args.append(pe)
    in_specs += [
        pl.BlockSpec((None, 1, 3 * D), mod_idx),
        pl.BlockSpec((None, D_A, D), lambda i: (layer, 0, 0)),
        pl.BlockSpec((None, D_B, D), lambda i: (layer, 0, 0)),
        pl.BlockSpec((None, D, D), lambda i: (layer, 0, 0)),
    ]
    args += [mod, wpa, wpb, wo]
    if final:
        in_specs.append(pl.BlockSpec((1, D), lambda i: (0, 0)))
        args.append(final_w)
    return pl.pallas_call(
        functools.partial(_merge_kernel, has_pe, final),
        grid=(tokens // tm,),
        in_specs=in_specs,
        out_specs=pl.BlockSpec((tm, D), lambda i: (i, 0)),
        out_shape=jax.ShapeDtypeStruct((tokens, D), F32),
        compiler_params=_cparams(1),
        name="merge",
    )(*args)


def kernel(x_prompt, x_sample, state_hgrn, c, c_ctx, norm_w, w_ada, b_ada, w_in, lb_raw,
           gnorm_w, w_pa, w_pb, w_o, final_norm_w):
    n_p, len_p, _ = x_prompt.shape
    n_s, len_s, _ = x_sample.shape

    cond = jnp.zeros((MOD_ROWS, D), F32).at[0].set(c_ctx).at[1:1 + n_s].set(c)
    mod = _modulation(cond, w_ada, b_ada)
    lbs = _lower_bounds(lb_raw)

    w_in_bf = w_in.astype(BF16)
    wpa_bf = w_pa.astype(BF16)
    wpb_bf = w_pb.astype(BF16)
    wo_bf = w_o.astype(BF16)
    norm_w3 = norm_w.reshape(DEPTH, 1, D)
    gnorm3 = gnorm_w.reshape(DEPTH, 1, D_A)
    final_w = final_norm_w.reshape(1, D)

    cc_np, sc_np = _dft_cos_sin(GROUP)
    cc = jnp.asarray(cc_np, F32)
    sc = jnp.asarray(sc_np, F32)
    pe = jnp.asarray(_pos_embed_table(len_s, D))

    xp = x_prompt.reshape(n_p * len_p, D)
    xs = x_sample.reshape(n_s * len_s, D)
    states = []
    for l in range(DEPTH):
        last = l == DEPTH - 1
        fw = final_w if last else None
        pe_l = pe if l == 0 else None

        pf, pa, zr, zi, zb = _proj(xp, None, mod, norm_w3, w_in_bf, cc, sc, l, len_p, False)
        ya, st = _scan(pf, pa, lbs, gnorm3, None, l, n_p, len_p, True)
        yb = _fourier_dense(zr, zi, zb, n_p, len_p)
        xp = _merge(ya, yb, pa, xp, None, mod, wpa_bf, wpb_bf, wo_bf, fw, l, len_p, False)
        states.append(st)

        pf, pa, zr, zi, zb = _proj(xs, pe_l, mod, norm_w3, w_in_bf, cc, sc, l, len_s, True)
        ya, _ = _scan(pf, pa, lbs, gnorm3, state_hgrn, l, n_s, len_s, False)
        yb = _fourier_two_stage(zr, zi, zb, n_s, len_s)
        xs = _merge(ya, yb, pa, xs, pe_l, mod, wpa_bf, wpb_bf, wo_bf, fw, l, len_s, True)

    y_prompt = xp.reshape(n_p, len_p, D)
    y_sample = xs.reshape(n_s, len_s, D)
    new_state = jnp.stack(states, axis=1)
    return (y_prompt, y_sample, new_state)
```

```python
import functools

import jax
import jax.numpy as jnp
import numpy as np
from jax import lax
from jax.experimental import pallas as pl
from jax.experimental.pallas import tpu as pltpu

D = 1024
DEPTH = 4
N_HEADS = 8
HEAD = 128
D_A = N_HEADS * HEAD
N_GROUPS = 4
GROUP = 128
D_B = N_GROUPS * GROUP
D_IN = 5 * D_A + 2 * D_B + 2 * D
GRID_W = 64
EPS = 1e-6

V7X_VMEM_LIMIT_BYTES = 56 * 1024 * 1024

SCAN_CHUNK = 64
SCAN_GROUP = 8
SCAN_UNROLL = 4
EXP_CLAMP = 80.0
TM_PROJ = 256
TM_MERGE = 512
FFT_RADIX = 64
FFT_BLOCK = 8
MOD_ROWS = 8

F32 = jnp.float32
BF16 = jnp.bfloat16


def _cparams(n_axes):
    return pltpu.CompilerParams(
        dimension_semantics=("arbitrary",) * n_axes,
        vmem_limit_bytes=V7X_VMEM_LIMIT_BYTES,
    )


def _dot(a, b):
    return jnp.dot(a, b, preferred_element_type=F32)


def _dot_nt(a, b):
    return lax.dot_general(a, b, (((1,), (1,)), ((), ())), preferred_element_type=F32)


def _dot_tn(a, b):
    return lax.dot_general(a, b, (((0,), (0,)), ((), ())), preferred_element_type=F32)


def _sigmoid(x):
    return 1.0 / (1.0 + jnp.exp(-x))


def _silu(x):
    return x * _sigmoid(x)


def _dft_cos_sin(n):
    k = np.arange(n, dtype=np.float64)
    ang = 2.0 * np.pi * np.outer(k, k) / n
    return np.cos(ang), np.sin(ang)


def _pos_embed_table(length, d):
    rows = length // GRID_W
    r = np.repeat(np.arange(rows, dtype=np.float64), GRID_W)
    col = np.tile(np.arange(GRID_W, dtype=np.float64), rows)
    nf = d // 4
    freqs = 1.0 / (10000.0 ** (np.arange(nf, dtype=np.float64) / nf))

    def emb(p):
        a = p[:, None] * freqs[None, :]
        return np.concatenate([np.sin(a), np.cos(a)], axis=-1)

    return np.concatenate([emb(r), emb(col)], axis=-1).astype(np.float32)


def _lb_kernel(raw_ref, out_ref):
    x = raw_ref[...]
    m = jnp.max(x, axis=0, keepdims=True)
    e = jnp.exp(x - m)
    p = e / jnp.sum(e, axis=0, keepdims=True)
    cs = p[0:1]
    first = cs
    out_ref[0:1, :] = cs - first
    for l in range(1, DEPTH):
        cs = cs + p[l:l + 1]
        out_ref[l:l + 1, :] = cs - first


def _lower_bounds(lb_raw):
    raw = lb_raw.reshape(DEPTH, 2 * D_A)
    out = pl.pallas_call(
        _lb_kernel,
        out_shape=jax.ShapeDtypeStruct((DEPTH, 2 * D_A), F32),
        name="lower_bounds",
    )(raw)
    return out.reshape(DEPTH * 2, 1, D_A)


def _mod_kernel(cond_ref, w_ref, b_ref, out_ref):
    a = _silu(cond_ref[...])
    out_ref[...] = jnp.dot(a, w_ref[...], preferred_element_type=F32,
                           precision=lax.Precision.HIGHEST) + b_ref[...]


def _modulation(cond, w_ada, b_ada):
    tn = 512
    out = pl.pallas_call(
        _mod_kernel,
        grid=(DEPTH, 3 * D // tn),
        in_specs=[
            pl.BlockSpec((MOD_ROWS, D), lambda l, j: (0, 0)),
            pl.BlockSpec((None, D, tn), lambda l, j: (l, 0, j)),
            pl.BlockSpec((None, 1, tn), lambda l, j: (l, 0, j)),
        ],
        out_specs=pl.BlockSpec((None, MOD_ROWS, tn), lambda l, j: (l, 0, j)),
        out_shape=jax.ShapeDtypeStruct((DEPTH, MOD_ROWS, 3 * D), F32),
        compiler_params=_cparams(2),
        name="modulation",
    )(cond, w_ada, b_ada.reshape(DEPTH, 1, 3 * D))
    return out.reshape(DEPTH * MOD_ROWS, 1, 3 * D)


_Q0, _FF0, _V0, _ZA0, _U0, _ZB0, _GA0, _GB0 = (
    0, D_A, 3 * D_A, 4 * D_A, 5 * D_A, 5 * D_A + D_B, 5 * D_A + 2 * D_B, 5 * D_A + 2 * D_B + D)
PA_Q, PA_V, PA_Z, PA_GA, PA_GB, PA_KF = 0, 1, 2, 3, 4, 5
PA_WIDTH = 7 * D


def _forget_gate(x, lb):
    f = lb + (1.0 - lb) * _sigmoid(x)
    log_f = jnp.where(f < 1e-30, jnp.minimum(x, 0.0), jnp.log(jnp.maximum(f, 1e-30)))
    return log_f, 1.0 - f


def _proj_kernel(has_pe, *refs):
    refs = list(refs)
    x = refs[0][...]
    pos = 1
    if has_pe:
        x = x + refs[pos][...]
        pos += 1
    mod_ref, nw_ref, lbf_ref, lbb_ref, w_ref, cc_ref, sc_ref = refs[pos:pos + 7]
    pf_ref, pa_ref, zr_ref, zi_ref, zb_ref = refs[pos + 7:]

    ms = jnp.mean(x * x, axis=-1, keepdims=True)
    y = x * lax.rsqrt(ms + EPS) * nw_ref[...]
    shift = mod_ref[:, 0:D]
    scale = mod_ref[:, D:2 * D]
    h = (y * (1.0 + scale) + shift).astype(BF16)

    def sec(c0, width):
        return _dot(h, w_ref[:, c0:c0 + width])

    def put(block, val):
        pa_ref[:, block * D:(block + 1) * D] = val.astype(BF16)

    tm = x.shape[0]
    row = lax.broadcasted_iota(jnp.int32, (tm, tm), 0)
    col = lax.broadcasted_iota(jnp.int32, (tm, tm), 1)
    same_chunk = (row // SCAN_CHUNK) == (col // SCAN_CHUNK)
    tris = (jnp.where(same_chunk & (col <= row), 1.0, 0.0).astype(BF16),
            jnp.where(same_chunk & (col >= row), 1.0, 0.0).astype(BF16))
    for d, lb_ref in enumerate((lbf_ref, lbb_ref)):
        g, kk = _forget_gate(sec(_FF0 + d * D_A, D_A), lb_ref[...])
        g_hi = g.astype(BF16)
        g_lo = (g - g_hi.astype(F32)).astype(BF16)
        pf_ref[:, d * D_A:(d + 1) * D_A] = _dot(tris[d], g_hi) + _dot(tris[d], g_lo)
        put(PA_KF + d, kk)

    put(PA_Q, _silu(sec(_Q0, D_A)))
    put(PA_V, sec(_V0, D_A))
    put(PA_Z, _silu(sec(_ZA0, D_A)))
    put(PA_GA, _sigmoid(sec(_GA0, D)))
    put(PA_GB, _sigmoid(sec(_GB0, D)))
    zb_ref[...] = _silu(sec(_ZB0, D_B)).astype(BF16)
    u = sec(_U0, D_B).astype(BF16)
    cc = cc_ref[...].astype(BF16)
    sc = sc_ref[...].astype(BF16)
    for g in range(N_GROUPS):
        ug = u[:, g * GROUP:(g + 1) * GROUP]
        zr_ref[:, g * GROUP:(g + 1) * GROUP] = _dot(ug, cc).astype(BF16)
        zi_ref[:, g * GROUP:(g + 1) * GROUP] = (-_dot(ug, sc)).astype(BF16)


def _proj(x, pe, mod, norm_w3, lbs, w_in_bf, cc, sc, layer, seq_len, latent):
    tokens = x.shape[0]
    tm = TM_PROJ
    has_pe = pe is not None
    per_seq = seq_len // tm
    if latent:
        mod_idx = lambda i: (layer * MOD_ROWS + 1 + i // per_seq, 0, 0)
    else:
        mod_idx = lambda i: (layer * MOD_ROWS, 0, 0)
    in_specs = [pl.BlockSpec((tm, D), lambda i: (i, 0))]
    args = [x]
    if has_pe:
        in_specs.append(pl.BlockSpec((tm, D), lambda i: (i % per_seq, 0)))
        args.append(pe)
    in_specs += [
        pl.BlockSpec((None, 1, 3 * D), mod_idx),
        pl.BlockSpec((None, 1, D), lambda i: (layer, 0, 0)),
        pl.BlockSpec((None, 1, D_A), lambda i: (2 * layer, 0, 0)),
        pl.BlockSpec((None, 1, D_A), lambda i: (2 * layer + 1, 0, 0)),
        pl.BlockSpec((None, D, D_IN), lambda i: (layer, 0, 0), pipeline_mode=pl.Buffered(1)),
        pl.BlockSpec((GROUP, GROUP), lambda i: (0, 0)),
        pl.BlockSpec((GROUP, GROUP), lambda i: (0, 0)),
    ]
    args += [mod, norm_w3, lbs, lbs, w_in_bf, cc, sc]
    out_shape = (
        jax.ShapeDtypeStruct((tokens, 2 * D_A), F32),
        jax.ShapeDtypeStruct((tokens, PA_WIDTH), BF16),
        jax.ShapeDtypeStruct((tokens, D_B), BF16),
        jax.ShapeDtypeStruct((tokens, D_B), BF16),
        jax.ShapeDtypeStruct((tokens, D_B), BF16),
    )
    out_specs = (
        pl.BlockSpec((tm, 2 * D_A), lambda i: (i, 0)),
        pl.BlockSpec((tm, PA_WIDTH), lambda i: (i, 0)),
        pl.BlockSpec((tm, D_B), lambda i: (i, 0)),
        pl.BlockSpec((tm, D_B), lambda i: (i, 0)),
        pl.BlockSpec((tm, D_B), lambda i: (i, 0)),
    )
    return pl.pallas_call(
        functools.partial(_proj_kernel, has_pe),
        grid=(tokens // tm,),
        in_specs=in_specs,
        out_specs=out_specs,
        out_shape=out_shape,
        compiler_params=_cparams(1),
        name="proj",
    )(*args)


def _scan_kernel(seq_len, has_init, emit_state, *refs):
    refs = list(refs)
    bf_ref, bb_ref, q_ref, v_ref, z_ref, kf_ref, kb_ref, gn_ref = refs[:8]
    pos = 8
    init_ref = None
    if has_init:
        init_ref = refs[pos]
        pos += 1
    y_ref = refs[pos]
    pos += 1
    st_ref = None
    if emit_state:
        st_ref = refs[pos]
        pos += 1
    (o_ref, of_ref, ob_ref, qinf_ref, qinb_ref, uf_ref, ub_ref, elf_ref, elb_ref) = refs[pos:pos + 9]

    C = SCAN_CHUNK
    half = C // 2
    n_chunks = seq_len // C
    G = min(SCAN_GROUP, n_chunks)
    GC = G * C

    rc = lax.broadcasted_iota(jnp.int32, (C, C), 0)
    cc = lax.broadcasted_iota(jnp.int32, (C, C), 1)
    dirs = (
        (bf_ref, kf_ref, cc <= rc, half - 1, C - 1, qinf_ref, uf_ref, elf_ref),
        (bb_ref, kb_ref, cc >= rc, half, 0, qinb_ref, ub_ref, elb_ref),
    )

    def local(j, carry):
        r0 = pl.multiple_of(j * GC, GC)
        q = q_ref[pl.ds(r0, GC), :].astype(F32)
        v = v_ref[pl.ds(r0, GC), :]
        work = []
        span = None
        for b_ref, k_ref, mask, mid_row, last_row, qin_ref, u_ref, el_ref in dirs:
            b = b_ref[pl.ds(r0, GC), :]
            kk = k_ref[pl.ds(r0, GC), :].astype(F32)
            for c in range(G):
                sl = slice(c * C, (c + 1) * C)
                bc = b[sl]
                r = bc[mid_row:mid_row + 1, :]
                b_last = bc[last_row:last_row + 1, :]
                ends = jnp.maximum(jnp.abs(bc[0:1, :] - r), jnp.abs(bc[C - 1:C, :] - r))
                span = ends if span is None else jnp.maximum(span, ends)
                eq = jnp.exp(jnp.clip(bc - r, -EXP_CLAMP, EXP_CLAMP))
                qt = q[sl] * eq
                kt = kk[sl] * (1.0 / eq)
                qin_ref[pl.ds(r0 + c * C, C), :] = (qt * jnp.exp(r)).astype(BF16)
                el_ref[j * G + c] = jnp.exp(b_last)
                k_out = (kt * jnp.exp(b_last - r)).astype(BF16)
                work.append((qt.astype(BF16), kt.astype(BF16), k_out, mask, v[sl], u_ref, c))
        atts = [_dot_nt(qt, kt) for qt, kt, _, _, _, _, _ in work]
        atts = [jnp.where(w[3], a, 0.0).astype(BF16) for w, a in zip(work, atts)]
        for (_, _, k_out, _, vc, u_ref, c) in work:
            u_ref[j * G + c] = _dot_tn(vc, k_out)
        outs = [_dot(a, w[4]) for w, a in zip(work, atts)]
        o_f = jnp.concatenate(outs[:G], axis=0)
        o_b = jnp.concatenate(outs[G:], axis=0)
        o_ref[pl.ds(r0, GC), :] = o_f + o_b

        @pl.when(jnp.max(span) > EXP_CLAMP)
        def _():
            def exact_chunk(c, carry):
                rows = pl.ds(pl.multiple_of(j * GC + c * C, C), C)
                span_c = None
                for b_ref, _, _, mid_row, _, _, _, _ in dirs:
                    r = b_ref[rows, :][mid_row:mid_row + 1, :]
                    for end_row in (0, C - 1):
                        dev = jnp.abs(b_ref[rows, :][end_row:end_row + 1, :] - r)
                        span_c = dev if span_c is None else jnp.maximum(span_c, dev)

                @pl.when(jnp.max(span_c) > EXP_CLAMP)
                def _():
                    qc = q_ref[rows, :].astype(F32)
                    vc = v_ref[rows, :]
                    vf = vc.astype(F32)
                    s_idx = lax.broadcasted_iota(jnp.int32, (C, 1), 0)
                    for d, (b_ref, k_ref, _, _, last_row, qin_ref, u_ref, _) in enumerate(dirs):
                        bc = b_ref[rows, :]
                        kc = k_ref[rows, :].astype(F32)
                        b_last = bc[last_row:last_row + 1, :]
                        qin_ref[rows, :] = (qc * jnp.exp(bc)).astype(BF16)
                        u_ref[j * G + c] = _dot_tn(vc, (kc * jnp.exp(b_last - bc)).astype(BF16))

                        def exact_row(t, o_blk, d=d, bc=bc, kc=kc):
                            sel = s_idx == t
                            b_t = jnp.sum(jnp.where(sel, bc, 0.0), axis=0, keepdims=True)
                            q_t = jnp.sum(jnp.where(sel, qc, 0.0), axis=0, keepdims=True)
                            allowed = (s_idx <= t) if d == 0 else (s_idx >= t)
                            decay = jnp.where(allowed, jnp.exp(jnp.where(allowed, b_t - bc, 0.0)), 0.0)
                            att_t = jnp.sum(decay * (q_t * kc), axis=1, keepdims=True)
                            o_t = jnp.sum(att_t * vf, axis=0, keepdims=True)
                            return jnp.where(sel, o_t, o_blk)

                        o_blk = lax.fori_loop(0, C, exact_row, jnp.zeros((C, HEAD), F32))
                        if d == 0:
                            o_ref[rows, :] = o_blk
                        else:
                            o_ref[rows, :] += o_blk

                return carry

            lax.fori_loop(0, G, exact_chunk, 0)

        return carry

    lax.fori_loop(0, seq_len // GC, local, 0)

    if has_init:
        s_init = (init_ref[0].T, init_ref[1].T)
    else:
        s_init = (jnp.zeros((HEAD, HEAD), F32), jnp.zeros((HEAD, HEAD), F32))

    def step(i, carry):
        s_f, s_b = carry
        jb = n_chunks - 1 - i
        rf = pl.multiple_of(i * C, C)
        rb = pl.multiple_of(jb * C, C)
        of_ref[pl.ds(rf, C), :] = _dot_nt(qinf_ref[pl.ds(rf, C), :], s_f.astype(BF16))
        ob_ref[pl.ds(rb, C), :] = _dot_nt(qinb_ref[pl.ds(rb, C), :], s_b.astype(BF16))
        return (s_f * elf_ref[i] + uf_ref[i], s_b * elb_ref[jb] + ub_ref[jb])

    s_f, s_b = lax.fori_loop(0, n_chunks, step, s_init, unroll=SCAN_UNROLL)

    gn = gn_ref[...]
    rows = min(256, seq_len)

    def finish(j, carry):
        r0 = pl.multiple_of(j * rows, rows)
        o = o_ref[pl.ds(r0, rows), :] + of_ref[pl.ds(r0, rows), :] + ob_ref[pl.ds(r0, rows), :]
        ms = jnp.mean(o * o, axis=-1, keepdims=True)
        o = o * lax.rsqrt(ms + EPS) * gn
        y_ref[pl.ds(r0, rows), :] = (o * z_ref[pl.ds(r0, rows), :].astype(F32)).astype(BF16)
        return carry

    lax.fori_loop(0, seq_len // rows, finish, 0)

    if emit_state:
        st_ref[0] = s_f.T
        st_ref[1] = s_b.T


def _scan(pf, pa, gnorm3, init_state, layer, n_seq, seq_len, emit_state):
    tokens = n_seq * seq_len
    n_chunks = seq_len // SCAN_CHUNK
    has_init = init_state is not None
    blk = lambda c0: pl.BlockSpec((seq_len, HEAD), lambda b, h, c0=c0: (b, c0 + h))
    in_specs = [
        blk(0), blk(N_HEADS),
        blk(PA_Q * N_HEADS), blk(PA_V * N_HEADS), blk(PA_Z * N_HEADS),
        blk(PA_KF * N_HEADS), blk((PA_KF + 1) * N_HEADS),
        pl.BlockSpec((None, 1, HEAD), lambda b, h: (layer, 0, h)),
    ]
    args = [pf, pf, pa, pa, pa, pa, pa, gnorm3]
    if has_init:
        in_specs.append(pl.BlockSpec((None, None, 2, None, HEAD, HEAD),
                                     lambda b, h: (b, layer, 0, h, 0, 0)))
        args.append(init_state)
    out_shape = [jax.ShapeDtypeStruct((tokens, D_A), BF16)]
    out_specs = [pl.BlockSpec((seq_len, HEAD), lambda b, h: (b, h))]
    if emit_state:
        out_shape.append(jax.ShapeDtypeStruct((n_seq, 2, N_HEADS, HEAD, HEAD), F32))
        out_specs.append(pl.BlockSpec((None, 2, None, HEAD, HEAD), lambda b, h: (b, 0, h, 0, 0)))
    res = pl.pallas_call(
        functools.partial(_scan_kernel, seq_len, has_init, emit_state),
        grid=(n_seq, N_HEADS),
        in_specs=in_specs,
        out_specs=out_specs,
        out_shape=out_shape,
        scratch_shapes=[
            pltpu.VMEM((seq_len, HEAD), F32),
            pltpu.VMEM((seq_len, HEAD), F32), pltpu.VMEM((seq_len, HEAD), F32),
            pltpu.VMEM((seq_len, HEAD), BF16), pltpu.VMEM((seq_len, HEAD), BF16),
            pltpu.VMEM((n_chunks, HEAD, HEAD), F32), pltpu.VMEM((n_chunks, HEAD, HEAD), F32),
            pltpu.VMEM((n_chunks, 1, HEAD), F32), pltpu.VMEM((n_chunks, 1, HEAD), F32),
        ],
        compiler_params=_cparams(2),
        name="scan",
    )(*args)
    return res if emit_state else (res[0], None)


def _fourier_dense_kernel(scale, zr_ref, zi_ref, zb_ref, c_ref, s_ref, y_ref):
    yf = (_dot(c_ref[...].astype(BF16), zr_ref[...])
          + _dot(s_ref[...].astype(BF16), zi_ref[...]))
    y_ref[...] = (yf * scale * zb_ref[...].astype(F32)).astype(BF16)


def _fourier_dense(zr, zi, zb, n_seq, seq_len):
    c, s = _dft_cos_sin(seq_len)
    scale = 1.0 / np.sqrt(seq_len * GROUP)
    blk = pl.BlockSpec((seq_len, D_B), lambda b: (b, 0))
    mat = pl.BlockSpec((seq_len, seq_len), lambda b: (0, 0))
    return pl.pallas_call(
        functools.partial(_fourier_dense_kernel, scale),
        grid=(n_seq,),
        in_specs=[blk, blk, blk, mat, mat],
        out_specs=blk,
        out_shape=jax.ShapeDtypeStruct((n_seq * seq_len, D_B), BF16),
        compiler_params=_cparams(1),
        name="fourier_dense",
    )(zr, zi, zb, jnp.asarray(c, F32), jnp.asarray(s, F32))


def _fourier_stage1_kernel(zr_ref, zi_ref, m_ref, tc_ref, ts_ref, br_ref, bi_ref):
    z = jnp.concatenate([zr_ref[...], zi_ref[...]], axis=0)
    a = _dot(m_ref[...].astype(BF16), z)
    ar = a[:FFT_RADIX]
    ai = a[FFT_RADIX:]
    tc = tc_ref[...]
    ts = ts_ref[...]
    br_ref[...] = (ar * tc + ai * ts).astype(BF16)
    bi_ref[...] = (ai * tc - ar * ts).astype(BF16)


def _fourier_stage2_kernel(scale, br_ref, bi_ref, zb_ref, cs_ref, y_ref):
    cs = cs_ref[...].astype(BF16)
    for j in range(FFT_BLOCK):
        bcat = jnp.concatenate([br_ref[j], bi_ref[j]], axis=0)
        yf = _dot(cs, bcat) * scale
        gate = zb_ref[:, j * D_B:(j + 1) * D_B].astype(F32)
        y_ref[:, j * D_B:(j + 1) * D_B] = (yf * gate).astype(BF16)


def _fourier_two_stage(zr, zi, zb, n_seq, seq_len):
    R = FFT_RADIX
    assert seq_len == R * R
    wide = R * D_B
    cols = FFT_BLOCK * D_B
    c, s = _dft_cos_sin(R)
    m1 = np.block([[c, s], [-s, c]])
    k1 = np.arange(R, dtype=np.float64)
    ang = 2.0 * np.pi * np.outer(k1, k1) / seq_len
    tc = np.repeat(np.cos(ang), D_B, axis=1).astype(np.float32)
    ts = np.repeat(np.sin(ang), D_B, axis=1).astype(np.float32)
    cs = np.concatenate([c, s], axis=1)
    scale = 1.0 / np.sqrt(seq_len * GROUP)

    zr3 = zr.reshape(n_seq, R, wide)
    zi3 = zi.reshape(n_seq, R, wide)
    seq_blk = pl.BlockSpec((None, R, cols), lambda b, j: (b, 0, j))
    br, bi = pl.pallas_call(
        _fourier_stage1_kernel,
        grid=(n_seq, R // FFT_BLOCK),
        in_specs=[seq_blk, seq_blk,
                  pl.BlockSpec((2 * R, 2 * R), lambda b, j: (0, 0)),
                  pl.BlockSpec((R, cols), lambda b, j: (0, j)),
                  pl.BlockSpec((R, cols), lambda b, j: (0, j))],
        out_specs=(seq_blk, seq_blk),
        out_shape=(jax.ShapeDtypeStruct((n_seq, R, wide), BF16),) * 2,
        compiler_params=_cparams(2),
        name="fourier_stage1",
    )(zr3, zi3, jnp.asarray(m1, F32), jnp.asarray(tc), jnp.asarray(ts))

    br4 = br.reshape(n_seq, R, R, D_B)
    bi4 = bi.reshape(n_seq, R, R, D_B)
    zb3 = zb.reshape(n_seq, R, wide)
    k_blk = pl.BlockSpec((None, FFT_BLOCK, R, D_B), lambda b, j: (b, j, 0, 0))
    y = pl.pallas_call(
        functools.partial(_fourier_stage2_kernel, scale),
        grid=(n_seq, R // FFT_BLOCK),
        in_specs=[k_blk, k_blk, seq_blk, pl.BlockSpec((R, 2 * R), lambda b, j: (0, 0))],
        out_specs=seq_blk,
        out_shape=jax.ShapeDtypeStruct((n_seq, R, wide), BF16),
        compiler_params=_cparams(2),
        name="fourier_stage2",
    )(br4, bi4, zb3, jnp.asarray(cs, F32))
    return y.reshape(n_seq * seq_len, D_B)


def _merge_kernel(has_pe, final, *refs):
    refs = list(refs)
    ya_ref, yb_ref, ga_ref, gb_ref, x_ref = refs[:5]
    pos = 5
    x = x_ref[...]
    if has_pe:
        x = x + refs[pos][...]
        pos += 1
    mod_ref, wpa_ref, wpb_ref, wo_ref = refs[pos:pos + 4]
    pos += 4
    if final:
        fw_ref = refs[pos]
        pos += 1
    out_ref = refs[pos]
    merged = (ga_ref[...].astype(F32) * _dot(ya_ref[...], wpa_ref[...])
              + gb_ref[...].astype(F32) * _dot(yb_ref[...], wpb_ref[...]))
    out = _dot(merged.astype(BF16), wo_ref[...])
    xn = x + mod_ref[:, 2 * D:3 * D] * out
    if final:
        ms = jnp.mean(xn * xn, axis=-1, keepdims=True)
        xn = xn * lax.rsqrt(ms + EPS) * fw_ref[...]
    out_ref[...] = xn


def _merge(ya, yb, pa, x, pe, mod, wpa, wpb, wo, final_w, layer, seq_len, latent):
    tokens = x.shape[0]
    tm = TM_MERGE
    has_pe = pe is not None
    final = final_w is not None
    row = lambda i: (i, 0)
    if latent:
        mod_idx = lambda i: (layer * MOD_ROWS + 1 + (i * tm) // seq_len, 0, 0)
    else:
        mod_idx = lambda i: (layer * MOD_ROWS, 0, 0)
    in_specs = [
        pl.BlockSpec((tm, D_A), row),
        pl.BlockSpec((tm, D_B), row),
        pl.BlockSpec((tm, D), lambda i: (i, PA_GA)),
        pl.BlockSpec((tm, D), lambda i: (i, PA_GB)),
        pl.BlockSpec((tm, D), row),
    ]
    args = [ya, yb, pa, pa, x]
    if has_pe:
        per_seq = seq_len // tm
        in_specs.append(pl.BlockSpec((tm, D), lambda i: (i % per_seq, 0)))
        args.append(pe)
    in_specs += [
        pl.BlockSpec((None, 1, 3 * D), mod_idx),
        pl.BlockSpec((None, D_A, D), lambda i: (layer, 0, 0)),
        pl.BlockSpec((None, D_B, D), lambda i: (layer, 0, 0)),
        pl.BlockSpec((None, D, D), lambda i: (layer, 0, 0)),
    ]
    args += [mod, wpa, wpb, wo]
    if final:
        in_specs.append(pl.BlockSpec((1, D), lambda i: (0, 0)))
        args.append(final_w)
    return pl.pallas_call(
        functools.partial(_merge_kernel, has_pe, final),
        grid=(tokens // tm,),
        in_specs=in_specs,
        out_specs=pl.BlockSpec((tm, D), row),
        out_shape=jax.ShapeDtypeStruct((tokens, D), F32),
        compiler_params=_cparams(1),
        name="merge",
    )(*args)


def kernel(x_prompt, x_sample, state_hgrn, c, c_ctx, norm_w, w_ada, b_ada, w_in, lb_raw,
           gnorm_w, w_pa, w_pb, w_o, final_norm_w):
    n_p, len_p, _ = x_prompt.shape
    n_s, len_s, _ = x_sample.shape

    cond = jnp.zeros((MOD_ROWS, D), F32).at[0].set(c_ctx).at[1:1 + n_s].set(c)
    mod = _modulation(cond, w_ada, b_ada)
    lbs = _lower_bounds(lb_raw)

    w_in_bf = w_in.astype(BF16)
    wpa_bf = w_pa.astype(BF16)
    wpb_bf = w_pb.astype(BF16)
    wo_bf = w_o.astype(BF16)
    norm_w3 = norm_w.reshape(DEPTH, 1, D)
    gnorm3 = gnorm_w.reshape(DEPTH, 1, D_A)
    final_w = final_norm_w.reshape(1, D)

    cc_np, sc_np = _dft_cos_sin(GROUP)
    cc = jnp.asarray(cc_np, F32)
    sc = jnp.asarray(sc_np, F32)
    pe = jnp.asarray(_pos_embed_table(len_s, D))

    xp = x_prompt.reshape(n_p * len_p, D)
    xs = x_sample.reshape(n_s * len_s, D)
    states = []
    for l in range(DEPTH):
        last = l == DEPTH - 1
        fw = final_w if last else None
        pe_l = pe if l == 0 else None

        pf, pa, zr, zi, zb = _proj(xp, None, mod, norm_w3, lbs, w_in_bf, cc, sc, l, len_p, False)
        ya, st = _scan(pf, pa, gnorm3, None, l, n_p, len_p, True)
        yb = _fourier_dense(zr, zi, zb, n_p, len_p)
        xp = _merge(ya, yb, pa, xp, None, mod, wpa_bf, wpb_bf, wo_bf, fw, l, len_p, False)
        states.append(st)

        pf, pa, zr, zi, zb = _proj(xs, pe_l, mod, norm_w3, lbs, w_in_bf, cc, sc, l, len_s, True)
        ya, _ = _scan(pf, pa, gnorm3, state_hgrn, l, n_s, len_s, False)
        yb = _fourier_two_stage(zr, zi, zb, n_s, len_s)
        xs = _merge(ya, yb, pa, xs, pe_l, mod, wpa_bf, wpb_bf, wo_bf, fw, l, len_s, True)

    y_prompt = xp.reshape(n_p, len_p, D)
    y_sample = xs.reshape(n_s, len_s, D)
    new_state = jnp.stack(states, axis=1)
    return (y_prompt, y_sample, new_state)
```

```python
import functools

import jax
import jax.numpy as jnp
import numpy as np
from jax import lax
from jax.experimental import pallas as pl
from jax.experimental.pallas import tpu as pltpu

D = 1024
DEPTH = 4
N_HEADS = 8
HEAD = 128
D_A = N_HEADS * HEAD
N_GROUPS = 4
GROUP = 128
D_B = N_GROUPS * GROUP
D_IN = 5 * D_A + 2 * D_B + 2 * D
GRID_W = 64
EPS = 1e-6

V7X_VMEM_LIMIT_BYTES = 56 * 1024 * 1024

SCAN_CHUNK = 64
SCAN_GROUP = 8
SCAN_UNROLL = 4
EXP_CLAMP = 80.0
TM_PROJ = 256
TM_MERGE = 512
FFT_RADIX = 64
FFT_BLOCK = 8
MOD_ROWS = 8

F32 = jnp.float32
BF16 = jnp.bfloat16


def _cparams(n_axes):
    return pltpu.CompilerParams(
        dimension_semantics=("arbitrary",) * n_axes,
        vmem_limit_bytes=V7X_VMEM_LIMIT_BYTES,
    )


def _dot(a, b):
    return jnp.dot(a, b, preferred_element_type=F32)


def _dot_nt(a, b):
    return lax.dot_general(a, b, (((1,), (1,)), ((), ())), preferred_element_type=F32)


def _dot_tn(a, b):
    return lax.dot_general(a, b, (((0,), (0,)), ((), ())), preferred_element_type=F32)


def _sigmoid(x):
    return 1.0 / (1.0 + jnp.exp(-x))


def _silu(x):
    return x * _sigmoid(x)


def _dft_cos_sin(n):
    k = np.arange(n, dtype=np.float64)
    ang = 2.0 * np.pi * np.outer(k, k) / n
    return np.cos(ang), np.sin(ang)


def _pos_embed_table(length, d):
    rows = length // GRID_W
    r = np.repeat(np.arange(rows, dtype=np.float64), GRID_W)
    col = np.tile(np.arange(GRID_W, dtype=np.float64), rows)
    nf = d // 4
    freqs = 1.0 / (10000.0 ** (np.arange(nf, dtype=np.float64) / nf))

    def emb(p):
        a = p[:, None] * freqs[None, :]
        return np.concatenate([np.sin(a), np.cos(a)], axis=-1)

    return np.concatenate([emb(r), emb(col)], axis=-1).astype(np.float32)


def _lb_kernel(raw_ref, out_ref):
    x = raw_ref[...]
    m = jnp.max(x, axis=0, keepdims=True)
    e = jnp.exp(x - m)
    p = e / jnp.sum(e, axis=0, keepdims=True)
    cs = p[0:1]
    first = cs
    out_ref[0:1, :] = cs - first
    for l in range(1, DEPTH):
        cs = cs + p[l:l + 1]
        out_ref[l:l + 1, :] = cs - first


def _lower_bounds(lb_raw):
    raw = lb_raw.reshape(DEPTH, 2 * D_A)
    out = pl.pallas_call(
        _lb_kernel,
        out_shape=jax.ShapeDtypeStruct((DEPTH, 2 * D_A), F32),
        name="lower_bounds",
    )(raw)
    return out.reshape(DEPTH * 2, 1, D_A)


def _mod_kernel(cond_ref, w_ref, b_ref, out_ref):
    a = _silu(cond_ref[...])
    out_ref[...] = jnp.dot(a, w_ref[...], preferred_element_type=F32,
                           precision=lax.Precision.HIGHEST) + b_ref[...]


def _modulation(cond, w_ada, b_ada):
    tn = 512
    out = pl.pallas_call(
        _mod_kernel,
        grid=(DEPTH, 3 * D // tn),
        in_specs=[
            pl.BlockSpec((MOD_ROWS, D), lambda l, j: (0, 0)),
            pl.BlockSpec((None, D, tn), lambda l, j: (l, 0, j)),
            pl.BlockSpec((None, 1, tn), lambda l, j: (l, 0, j)),
        ],
        out_specs=pl.BlockSpec((None, MOD_ROWS, tn), lambda l, j: (l, 0, j)),
        out_shape=jax.ShapeDtypeStruct((DEPTH, MOD_ROWS, 3 * D), F32),
        compiler_params=_cparams(2),
        name="modulation",
    )(cond, w_ada, b_ada.reshape(DEPTH, 1, 3 * D))
    return out.reshape(DEPTH * MOD_ROWS, 1, 3 * D)


_Q0, _FF0, _V0, _ZA0, _U0, _ZB0, _GA0, _GB0 = (
    0, D_A, 3 * D_A, 4 * D_A, 5 * D_A, 5 * D_A + D_B, 5 * D_A + 2 * D_B, 5 * D_A + 2 * D_B + D)
PA_Q, PA_V, PA_Z, PA_GA, PA_GB, PA_KF = 0, 1, 2, 3, 4, 5
PA_WIDTH = 7 * D


def _forget_gate(x, lb):
    f = lb + (1.0 - lb) * _sigmoid(x)
    log_f = jnp.where(f < 1e-30, jnp.minimum(x, 0.0), jnp.log(jnp.maximum(f, 1e-30)))
    return log_f, 1.0 - f


def _proj_kernel(has_pe, *refs):
    refs = list(refs)
    x = refs[0][...]
    pos = 1
    if has_pe:
        x = x + refs[pos][...]
        pos += 1
    mod_ref, nw_ref, lbf_ref, lbb_ref, w_ref, cc_ref, sc_ref = refs[pos:pos + 7]
    pf_ref, pa_ref, zr_ref, zi_ref, zb_ref = refs[pos + 7:]

    ms = jnp.mean(x * x, axis=-1, keepdims=True)
    y = x * lax.rsqrt(ms + EPS) * nw_ref[...]
    shift = mod_ref[:, 0:D]
    scale = mod_ref[:, D:2 * D]
    h = (y * (1.0 + scale) + shift).astype(BF16)

    def sec(c0, width):
        return _dot(h, w_ref[:, c0:c0 + width])

    def put(block, val):
        pa_ref[:, block * D:(block + 1) * D] = val.astype(BF16)

    tm = x.shape[0]
    row = lax.broadcasted_iota(jnp.int32, (tm, tm), 0)
    col = lax.broadcasted_iota(jnp.int32, (tm, tm), 1)
    same_chunk = (row // SCAN_CHUNK) == (col // SCAN_CHUNK)
    tris = (jnp.where(same_chunk & (col <= row), 1.0, 0.0).astype(BF16),
            jnp.where(same_chunk & (col >= row), 1.0, 0.0).astype(BF16))
    for d, lb_ref in enumerate((lbf_ref, lbb_ref)):
        g, kk = _forget_gate(sec(_FF0 + d * D_A, D_A), lb_ref[...])
        g_hi = g.astype(BF16)
        g_lo = (g - g_hi.astype(F32)).astype(BF16)
        pf_ref[:, d * D_A:(d + 1) * D_A] = _dot(tris[d], g_hi) + _dot(tris[d], g_lo)
        put(PA_KF + d, kk)

    put(PA_Q, _silu(sec(_Q0, D_A)))
    put(PA_V, sec(_V0, D_A))
    put(PA_Z, _silu(sec(_ZA0, D_A)))
    put(PA_GA, _sigmoid(sec(_GA0, D)))
    put(PA_GB, _sigmoid(sec(_GB0, D)))
    zb_ref[...] = _silu(sec(_ZB0, D_B)).astype(BF16)
    u = sec(_U0, D_B).astype(BF16)
    cc = cc_ref[...].astype(BF16)
    sc = sc_ref[...].astype(BF16)
    for g in range(N_GROUPS):
        ug = u[:, g * GROUP:(g + 1) * GROUP]
        zr_ref[:, g * GROUP:(g + 1) * GROUP] = _dot(ug, cc).astype(BF16)
        zi_ref[:, g * GROUP:(g + 1) * GROUP] = (-_dot(ug, sc)).astype(BF16)


def _proj(x, pe, mod, norm_w3, lbs, w_in_bf, cc, sc, layer, seq_len, latent):
    tokens = x.shape[0]
    tm = TM_PROJ
    has_pe = pe is not None
    per_seq = seq_len // tm
    if latent:
        mod_idx = lambda i: (layer * MOD_ROWS + 1 + i // per_seq, 0, 0)
    else:
        mod_idx = lambda i: (layer * MOD_ROWS, 0, 0)
    in_specs = [pl.BlockSpec((tm, D), lambda i: (i, 0))]
    args = [x]
    if has_pe:
        in_specs.append(pl.BlockSpec((tm, D), lambda i: (i % per_seq, 0)))
        args.append(pe)
    in_specs += [
        pl.BlockSpec((None, 1, 3 * D), mod_idx),
        pl.BlockSpec((None, 1, D), lambda i: (layer, 0, 0)),
        pl.BlockSpec((None, 1, D_A), lambda i: (2 * layer, 0, 0)),
        pl.BlockSpec((None, 1, D_A), lambda i: (2 * layer + 1, 0, 0)),
        pl.BlockSpec((None, D, D_IN), lambda i: (layer, 0, 0), pipeline_mode=pl.Buffered(1)),
        pl.BlockSpec((GROUP, GROUP), lambda i: (0, 0)),
        pl.BlockSpec((GROUP, GROUP), lambda i: (0, 0)),
    ]
    args += [mod, norm_w3, lbs, lbs, w_in_bf, cc, sc]
    out_shape = (
        jax.ShapeDtypeStruct((tokens, 2 * D_A), F32),
        jax.ShapeDtypeStruct((tokens, PA_WIDTH), BF16),
        jax.ShapeDtypeStruct((tokens, D_B), BF16),
        jax.ShapeDtypeStruct((tokens, D_B), BF16),
        jax.ShapeDtypeStruct((tokens, D_B), BF16),
    )
    out_specs = (
        pl.BlockSpec((tm, 2 * D_A), lambda i: (i, 0)),
        pl.BlockSpec((tm, PA_WIDTH), lambda i: (i, 0)),
        pl.BlockSpec((tm, D_B), lambda i: (i, 0)),
        pl.BlockSpec((tm, D_B), lambda i: (i, 0)),
        pl.BlockSpec((tm, D_B), lambda i: (i, 0)),
    )
    return pl.pallas_call(
        functools.partial(_proj_kernel, has_pe),
        grid=(tokens // tm,),
        in_specs=in_specs,
        out_specs=out_specs,
        out_shape=out_shape,
        compiler_params=_cparams(1),
        name="proj",
    )(*args)


def _scan_kernel(seq_len, has_init, emit_state, *refs):
    refs = list(refs)
    bf_ref, bb_ref, q_ref, v_ref, z_ref, kf_ref, kb_ref, gn_ref = refs[:8]
    pos = 8
    init_ref = None
    if has_init:
        init_ref = refs[pos]
        pos += 1
    y_ref = refs[pos]
    pos += 1
    st_ref = None
    if emit_state:
        st_ref = refs[pos]
        pos += 1
    (o_ref, of_ref, ob_ref, qinf_ref, qinb_ref, uf_ref, ub_ref, elf_ref, elb_ref) = refs[pos:pos + 9]

    C = SCAN_CHUNK
    H = C // 2
    n_chunks = seq_len // C
    G = min(SCAN_GROUP, n_chunks)
    GC = G * C

    def pair_masks(n):
        rr = lax.broadcasted_iota(jnp.int32, (n, n), 0)
        cc = lax.broadcasted_iota(jnp.int32, (n, n), 1)
        return cc <= rr, cc >= rr

    mask_c = pair_masks(C)
    mask_h = pair_masks(H)
    dirs = (
        (bf_ref, kf_ref, C - 1, qinf_ref, uf_ref, elf_ref),
        (bb_ref, kb_ref, 0, qinb_ref, ub_ref, elb_ref),
    )

    def split_factors(bq, qv, kv):
        n = bq.shape[0]
        r = 0.5 * (bq[0:1, :] + bq[n - 1:n, :])
        e = jnp.exp(jnp.clip(bq - r, -EXP_CLAMP, EXP_CLAMP))
        return qv * e, kv * (1.0 / e), r, jnp.abs(bq[0:1, :] - r)

    def halves(d, bc, qc, kc, vc):
        first, second = (slice(0, H), slice(H, C)) if d == 0 else (slice(H, C), slice(0, H))
        edge = bc[H - 1:H, :] if d == 0 else bc[H:H + 1, :]
        atts, dev = [], None
        for rows in (first, second):
            qt, kt, _, dev_h = split_factors(bc[rows], qc[rows], kc[rows])
            att = _dot_nt(qt.astype(BF16), kt.astype(BF16))
            atts.append(jnp.where(mask_h[d], att, 0.0).astype(BF16))
            dev = dev_h if dev is None else jnp.maximum(dev, dev_h)
        cross = _dot_nt((qc[second] * jnp.exp(bc[second] - edge)).astype(BF16),
                        (kc[first] * jnp.exp(edge - bc[first])).astype(BF16)).astype(BF16)
        o_first = _dot(atts[0], vc[first])
        o_second = _dot(atts[1], vc[second]) + _dot(cross, vc[first])
        parts = [o_first, o_second] if d == 0 else [o_second, o_first]
        return jnp.concatenate(parts, axis=0), dev

    def pairwise(d, bc, qc, kc, vf):
        s_idx = lax.broadcasted_iota(jnp.int32, (C, 1), 0)

        def row(t, o_blk):
            sel = s_idx == t
            b_t = jnp.sum(jnp.where(sel, bc, 0.0), axis=0, keepdims=True)
            q_t = jnp.sum(jnp.where(sel, qc, 0.0), axis=0, keepdims=True)
            allowed = (s_idx <= t) if d == 0 else (s_idx >= t)
            decay = jnp.where(allowed, jnp.exp(jnp.where(allowed, b_t - bc, 0.0)), 0.0)
            att_t = jnp.sum(decay * (q_t * kc), axis=1, keepdims=True)
            o_t = jnp.sum(att_t * vf, axis=0, keepdims=True)
            return jnp.where(sel, o_t, o_blk)

        return lax.fori_loop(0, C, row, jnp.zeros((C, HEAD), F32))

    def redo_chunk(chunk, rows):
        bcs = [b_ref[rows, :] for b_ref, _, _, _, _, _ in dirs]
        span = None
        for bc in bcs:
            r = 0.5 * (bc[0:1, :] + bc[C - 1:C, :])
            dev = jnp.maximum(jnp.abs(bc[0:1, :] - r), jnp.abs(r))
            span = dev if span is None else jnp.maximum(span, dev)

        @pl.when(jnp.max(span) > EXP_CLAMP)
        def _():
            qc = q_ref[rows, :].astype(F32)
            vc = v_ref[rows, :]
            kcs = [k_ref[rows, :].astype(F32) for _, k_ref, _, _, _, _ in dirs]
            o_sum, span_h = None, None
            for d, (_, _, last_row, qin_ref, u_ref, _) in enumerate(dirs):
                bc, kc = bcs[d], kcs[d]
                b_last = bc[last_row:last_row + 1, :]
                qin_ref[rows, :] = (qc * jnp.exp(bc)).astype(BF16)
                u_ref[chunk] = _dot_tn(vc, (kc * jnp.exp(b_last - bc)).astype(BF16))
                o_d, dev = halves(d, bc, qc, kc, vc)
                o_sum = o_d if o_sum is None else o_sum + o_d
                span_h = dev if span_h is None else jnp.maximum(span_h, dev)
            o_ref[rows, :] = o_sum

            @pl.when(jnp.max(span_h) > EXP_CLAMP)
            def _():
                vf = vc.astype(F32)
                o_ref[rows, :] = pairwise(0, bcs[0], qc, kcs[0], vf) + pairwise(1, bcs[1], qc, kcs[1], vf)

    def local(j, carry):
        r0 = pl.multiple_of(j * GC, GC)
        q = q_ref[pl.ds(r0, GC), :].astype(F32)
        v = v_ref[pl.ds(r0, GC), :]
        work = []
        span = None
        for d, (b_ref, k_ref, last_row, qin_ref, u_ref, el_ref) in enumerate(dirs):
            b = b_ref[pl.ds(r0, GC), :]
            kk = k_ref[pl.ds(r0, GC), :].astype(F32)
            for c in range(G):
                sl = slice(c * C, (c + 1) * C)
                bc = b[sl]
                b_last = bc[last_row:last_row + 1, :]
                qt, kt, r, dev = split_factors(bc, q[sl], kk[sl])
                dev = jnp.maximum(dev, jnp.abs(r))
                span = dev if span is None else jnp.maximum(span, dev)
                qin_ref[pl.ds(r0 + c * C, C), :] = (qt * jnp.exp(r)).astype(BF16)
                el_ref[j * G + c] = jnp.exp(b_last)
                k_out = (kt * jnp.exp(b_last - r)).astype(BF16)
                work.append((qt.astype(BF16), kt.astype(BF16), k_out, mask_c[d], v[sl], u_ref, c))
        atts = [_dot_nt(qt, kt) for qt, kt, _, _, _, _, _ in work]
        atts = [jnp.where(w[3], a, 0.0).astype(BF16) for w, a in zip(work, atts)]
        for (_, _, k_out, _, vc, u_ref, c) in work:
            u_ref[j * G + c] = _dot_tn(vc, k_out)
        outs = [_dot(a, w[4]) for w, a in zip(work, atts)]
        o_f = jnp.concatenate(outs[:G], axis=0)
        o_b = jnp.concatenate(outs[G:], axis=0)
        o_ref[pl.ds(r0, GC), :] = o_f + o_b

        @pl.when(jnp.max(span) > EXP_CLAMP)
        def _():
            def redo(c, carry):
                redo_chunk(j * G + c, pl.ds(pl.multiple_of(j * GC + c * C, C), C))
                return carry

            lax.fori_loop(0, G, redo, 0)

        return carry

    lax.fori_loop(0, seq_len // GC, local, 0)

    if has_init:
        s_init = (init_ref[0].T, init_ref[1].T)
    else:
        s_init = (jnp.zeros((HEAD, HEAD), F32), jnp.zeros((HEAD, HEAD), F32))

    def step(i, carry):
        s_f, s_b = carry
        jb = n_chunks - 1 - i
        rf = pl.multiple_of(i * C, C)
        rb = pl.multiple_of(jb * C, C)
        of_ref[pl.ds(rf, C), :] = _dot_nt(qinf_ref[pl.ds(rf, C), :], s_f.astype(BF16))
        ob_ref[pl.ds(rb, C), :] = _dot_nt(qinb_ref[pl.ds(rb, C), :], s_b.astype(BF16))
        return (s_f * elf_ref[i] + uf_ref[i], s_b * elb_ref[jb] + ub_ref[jb])

    s_f, s_b = lax.fori_loop(0, n_chunks, step, s_init, unroll=SCAN_UNROLL)

    gn = gn_ref[...]
    rows = min(256, seq_len)

    def finish(j, carry):
        r0 = pl.multiple_of(j * rows, rows)
        o = o_ref[pl.ds(r0, rows), :] + of_ref[pl.ds(r0, rows), :] + ob_ref[pl.ds(r0, rows), :]
        ms = jnp.mean(o * o, axis=-1, keepdims=True)
        o = o * lax.rsqrt(ms + EPS) * gn
        y_ref[pl.ds(r0, rows), :] = (o * z_ref[pl.ds(r0, rows), :].astype(F32)).astype(BF16)
        return carry

    lax.fori_loop(0, seq_len // rows, finish, 0)

    if emit_state:
        st_ref[0] = s_f.T
        st_ref[1] = s_b.T


def _scan(pf, pa, gnorm3, init_state, layer, n_seq, seq_len, emit_state):
    tokens = n_seq * seq_len
    n_chunks = seq_len // SCAN_CHUNK
    has_init = init_state is not None
    blk = lambda c0: pl.BlockSpec((seq_len, HEAD), lambda b, h, c0=c0: (b, c0 + h))
    in_specs = [
        blk(0), blk(N_HEADS),
        blk(PA_Q * N_HEADS), blk(PA_V * N_HEADS), blk(PA_Z * N_HEADS),
        blk(PA_KF * N_HEADS), blk((PA_KF + 1) * N_HEADS),
        pl.BlockSpec((None, 1, HEAD), lambda b, h: (layer, 0, h)),
    ]
    args = [pf, pf, pa, pa, pa, pa, pa, gnorm3]
    if has_init:
        in_specs.append(pl.BlockSpec((None, None, 2, None, HEAD, HEAD),
                                     lambda b, h: (b, layer, 0, h, 0, 0)))
        args.append(init_state)
    out_shape = [jax.ShapeDtypeStruct((tokens, D_A), BF16)]
    out_specs = [pl.BlockSpec((seq_len, HEAD), lambda b, h: (b, h))]
    if emit_state:
        out_shape.append(jax.ShapeDtypeStruct((n_seq, 2, N_HEADS, HEAD, HEAD), F32))
        out_specs.append(pl.BlockSpec((None, 2, None, HEAD, HEAD), lambda b, h: (b, 0, h, 0, 0)))
    res = pl.pallas_call(
        functools.partial(_scan_kernel, seq_len, has_init, emit_state),
        grid=(n_seq, N_HEADS),
        in_specs=in_specs,
        out_specs=out_specs,
        out_shape=out_shape,
        scratch_shapes=[
            pltpu.VMEM((seq_len, HEAD), F32),
            pltpu.VMEM((seq_len, HEAD), F32), pltpu.VMEM((seq_len, HEAD), F32),
            pltpu.VMEM((seq_len, HEAD), BF16), pltpu.VMEM((seq_len, HEAD), BF16),
            pltpu.VMEM((n_chunks, HEAD, HEAD), F32), pltpu.VMEM((n_chunks, HEAD, HEAD), F32),
            pltpu.VMEM((n_chunks, 1, HEAD), F32), pltpu.VMEM((n_chunks, 1, HEAD), F32),
        ],
        compiler_params=_cparams(2),
        name="scan",
    )(*args)
    return res if emit_state else (res[0], None)


def _fourier_dense_kernel(scale, zr_ref, zi_ref, zb_ref, c_ref, s_ref, y_ref):
    yf = (_dot(c_ref[...].astype(BF16), zr_ref[...])
          + _dot(s_ref[...].astype(BF16), zi_ref[...]))
    y_ref[...] = (yf * scale * zb_ref[...].astype(F32)).astype(BF16)


def _fourier_dense(zr, zi, zb, n_seq, seq_len):
    c, s = _dft_cos_sin(seq_len)
    scale = 1.0 / np.sqrt(seq_len * GROUP)
    blk = pl.BlockSpec((seq_len, D_B), lambda b: (b, 0))
    mat = pl.BlockSpec((seq_len, seq_len), lambda b: (0, 0))
    return pl.pallas_call(
        functools.partial(_fourier_dense_kernel, scale),
        grid=(n_seq,),
        in_specs=[blk, blk, blk, mat, mat],
        out_specs=blk,
        out_shape=jax.ShapeDtypeStruct((n_seq * seq_len, D_B), BF16),
        compiler_params=_cparams(1),
        name="fourier_dense",
    )(zr, zi, zb, jnp.asarray(c, F32), jnp.asarray(s, F32))


def _fourier_stage1_kernel(zr_ref, zi_ref, m_ref, tc_ref, ts_ref, br_ref, bi_ref):
    z = jnp.concatenate([zr_ref[...], zi_ref[...]], axis=0)
    a = _dot(m_ref[...].astype(BF16), z)
    ar = a[:FFT_RADIX]
    ai = a[FFT_RADIX:]
    tc = tc_ref[...]
    ts = ts_ref[...]
    br_ref[...] = (ar * tc + ai * ts).astype(BF16)
    bi_ref[...] = (ai * tc - ar * ts).astype(BF16)


def _fourier_stage2_kernel(scale, br_ref, bi_ref, zb_ref, cs_ref, y_ref):
    cs = cs_ref[...].astype(BF16)
    for j in range(FFT_BLOCK):
        bcat = jnp.concatenate([br_ref[j], bi_ref[j]], axis=0)
        yf = _dot(cs, bcat) * scale
        gate = zb_ref[:, j * D_B:(j + 1) * D_B].astype(F32)
        y_ref[:, j * D_B:(j + 1) * D_B] = (yf * gate).astype(BF16)


def _fourier_two_stage(zr, zi, zb, n_seq, seq_len):
    R = FFT_RADIX
    assert seq_len == R * R
    wide = R * D_B
    cols = FFT_BLOCK * D_B
    c, s = _dft_cos_sin(R)
    m1 = np.block([[c, s], [-s, c]])
    k1 = np.arange(R, dtype=np.float64)
    ang = 2.0 * np.pi * np.outer(k1, k1) / seq_len
    tc = np.repeat(np.cos(ang), D_B, axis=1).astype(np.float32)
    ts = np.repeat(np.sin(ang), D_B, axis=1).astype(np.float32)
    cs = np.concatenate([c, s], axis=1)
    scale = 1.0 / np.sqrt(seq_len * GROUP)

    zr3 = zr.reshape(n_seq, R, wide)
    zi3 = zi.reshape(n_seq, R, wide)
    seq_blk = pl.BlockSpec((None, R, cols), lambda b, j: (b, 0, j))
    br, bi = pl.pallas_call(
        _fourier_stage1_kernel,
        grid=(n_seq, R // FFT_BLOCK),
        in_specs=[seq_blk, seq_blk,
                  pl.BlockSpec((2 * R, 2 * R), lambda b, j: (0, 0)),
                  pl.BlockSpec((R, cols), lambda b, j: (0, j)),
                  pl.BlockSpec((R, cols), lambda b, j: (0, j))],
        out_specs=(seq_blk, seq_blk),
        out_shape=(jax.ShapeDtypeStruct((n_seq, R, wide), BF16),) * 2,
        compiler_params=_cparams(2),
        name="fourier_stage1",
    )(zr3, zi3, jnp.asarray(m1, F32), jnp.asarray(tc), jnp.asarray(ts))

    br4 = br.reshape(n_seq, R, R, D_B)
    bi4 = bi.reshape(n_seq, R, R, D_B)
    zb3 = zb.reshape(n_seq, R, wide)
    k_blk = pl.BlockSpec((None, FFT_BLOCK, R, D_B), lambda b, j: (b, j, 0, 0))
    y = pl.pallas_call(
        functools.partial(_fourier_stage2_kernel, scale),
        grid=(n_seq, R // FFT_BLOCK),
        in_specs=[k_blk, k_blk, seq_blk, pl.BlockSpec((R, 2 * R), lambda b, j: (0, 0))],
        out_specs=seq_blk,
        out_shape=jax.ShapeDtypeStruct((n_seq, R, wide), BF16),
        compiler_params=_cparams(2),
        name="fourier_stage2",
    )(br4, bi4, zb3, jnp.asarray(cs, F32))
    return y.reshape(n_seq * seq_len, D_B)


def _merge_kernel(has_pe, final, *refs):
    refs = list(refs)
    ya_ref, yb_ref, ga_ref, gb_ref, x_ref = refs[:5]
    pos = 5
    x = x_ref[...]
    if has_pe:
        x = x + refs[pos][...]
        pos += 1
    mod_ref, wpa_ref, wpb_ref, wo_ref = refs[pos:pos + 4]
    pos += 4
    if final:
        fw_ref = refs[pos]
        pos += 1
    out_ref = refs[pos]
    merged = (ga_ref[...].astype(F32) * _dot(ya_ref[...], wpa_ref[...])
              + gb_ref[...].astype(F32) * _dot(yb_ref[...], wpb_ref[...]))
    out = _dot(merged.astype(BF16), wo_ref[...])
    xn = x + mod_ref[:, 2 * D:3 * D] * out
    if final:
        ms = jnp.mean(xn * xn, axis=-1, keepdims=True)
        xn = xn * lax.rsqrt(ms + EPS) * fw_ref[...]
    out_ref[...] = xn


def _merge(ya, yb, pa, x, pe, mod, wpa, wpb, wo, final_w, layer, seq_len, latent):
    tokens = x.shape[0]
    tm = TM_MERGE
    has_pe = pe is not None
    final = final_w is not None
    row = lambda i: (i, 0)
    if latent:
        mod_idx = lambda i: (layer * MOD_ROWS + 1 + (i * tm) // seq_len, 0, 0)
    else:
        mod_idx = lambda i: (layer * MOD_ROWS, 0, 0)
    in_specs = [
        pl.BlockSpec((tm, D_A), row),
        pl.BlockSpec((tm, D_B), row),
        pl.BlockSpec((tm, D), lambda i: (i, PA_GA)),
        pl.BlockSpec((tm, D), lambda i: (i, PA_GB)),
        pl.BlockSpec((tm, D), row),
    ]
    args = [ya, yb, pa, pa, x]
    if has_pe:
        per_seq = seq_len // tm
        in_specs.append(pl.BlockSpec((tm, D), lambda i: (i % per_seq, 0)))
        args.append(pe)
    in_specs += [
        pl.BlockSpec((None, 1, 3 * D), mod_idx),
        pl.BlockSpec((None, D_A, D), lambda i: (layer, 0, 0)),
        pl.BlockSpec((None, D_B, D), lambda i: (layer, 0, 0)),
        pl.BlockSpec((None, D, D), lambda i: (layer, 0, 0)),
    ]
    args += [mod, wpa, wpb, wo]
    if final:
        in_specs.append(pl.BlockSpec((1, D), lambda i: (0, 0)))
        args.append(final_w)
    return pl.pallas_call(
        functools.partial(_merge_kernel, has_pe, final),
        grid=(tokens // tm,),
        in_specs=in_specs,
        out_specs=pl.BlockSpec((tm, D), row),
        out_shape=jax.ShapeDtypeStruct((tokens, D), F32),
        compiler_params=_cparams(1),
        name="merge",
    )(*args)


def kernel(x_prompt, x_sample, state_hgrn, c, c_ctx, norm_w, w_ada, b_ada, w_in, lb_raw,
           gnorm_w, w_pa, w_pb, w_o, final_norm_w):
    n_p, len_p, _ = x_prompt.shape
    n_s, len_s, _ = x_sample.shape

    cond = jnp.zeros((MOD_ROWS, D), F32).at[0].set(c_ctx).at[1:1 + n_s].set(c)
    mod = _modulation(cond, w_ada, b_ada)
    lbs = _lower_bounds(lb_raw)

    w_in_bf = w_in.astype(BF16)
    wpa_bf = w_pa.astype(BF16)
    wpb_bf = w_pb.astype(BF16)
    wo_bf = w_o.astype(BF16)
    norm_w3 = norm_w.reshape(DEPTH, 1, D)
    gnorm3 = gnorm_w.reshape(DEPTH, 1, D_A)
    final_w = final_norm_w.reshape(1, D)

    cc_np, sc_np = _dft_cos_sin(GROUP)
    cc = jnp.asarray(cc_np, F32)
    sc = jnp.asarray(sc_np, F32)
    pe = jnp.asarray(_pos_embed_table(len_s, D))

    xp = x_prompt.reshape(n_p * len_p, D)
    xs = x_sample.reshape(n_s * len_s, D)
    states = []
    for l in range(DEPTH):
        last = l == DEPTH - 1
        fw = final_w if last else None
        pe_l = pe if l == 0 else None

        pf, pa, zr, zi, zb = _proj(xp, None, mod, norm_w3, lbs, w_in_bf, cc, sc, l, len_p, False)
        ya, st = _scan(pf, pa, gnorm3, None, l, n_p, len_p, True)
        yb = _fourier_dense(zr, zi, zb, n_p, len_p)
        xp = _merge(ya, yb, pa, xp, None, mod, wpa_bf, wpb_bf, wo_bf, fw, l, len_p, False)
        states.append(st)

        pf, pa, zr, zi, zb = _proj(xs, pe_l, mod, norm_w3, lbs, w_in_bf, cc, sc, l, len_s, True)
        ya, _ = _scan(pf, pa, gnorm3, state_hgrn, l, n_s, len_s, False)
        yb = _fourier_two_stage(zr, zi, zb, n_s, len_s)
        xs = _merge(ya, yb, pa, xs, pe_l, mod, wpa_bf, wpb_bf, wo_bf, fw, l, len_s, True)

    y_prompt = xp.reshape(n_p, len_p, D)
    y_sample = xs.reshape(n_s, len_s, D)
    new_state = jnp.stack(states, axis=1)
    return (y_prompt, y_sample, new_state)
```

```python
import functools

import jax
import jax.numpy as jnp
import numpy as np
from jax import lax
from jax.experimental import pallas as pl
from jax.experimental.pallas import tpu as pltpu

D = 1024
DEPTH = 4
N_HEADS = 8
HEAD = 128
D_A = N_HEADS * HEAD
N_GROUPS = 4
GROUP = 128
D_B = N_GROUPS * GROUP
D_IN = 5 * D_A + 2 * D_B + 2 * D
GRID_W = 64
EPS = 1e-6

V7X_VMEM_LIMIT_BYTES = 56 * 1024 * 1024

SCAN_CHUNK = 64
SCAN_GROUP = 8
SCAN_UNROLL = 4
EXP_CLAMP = 80.0
TM_PROJ = 256
TM_MERGE = 512
FFT_RADIX = 64
FFT_BLOCK = 8
MOD_ROWS = 8

F32 = jnp.float32
BF16 = jnp.bfloat16


def _cparams(n_axes):
    return pltpu.CompilerParams(
        dimension_semantics=("arbitrary",) * n_axes,
        vmem_limit_bytes=V7X_VMEM_LIMIT_BYTES,
    )


def _dot(a, b):
    return jnp.dot(a, b, preferred_element_type=F32)


def _dot_nt(a, b):
    return lax.dot_general(a, b, (((1,), (1,)), ((), ())), preferred_element_type=F32)


def _dot_tn(a, b):
    return lax.dot_general(a, b, (((0,), (0,)), ((), ())), preferred_element_type=F32)


def _sigmoid(x):
    return 1.0 / (1.0 + jnp.exp(-x))


def _silu(x):
    return x * _sigmoid(x)


def _dft_cos_sin(n):
    k = np.arange(n, dtype=np.float64)
    ang = 2.0 * np.pi * np.outer(k, k) / n
    return np.cos(ang), np.sin(ang)


def _pos_embed_table(length, d):
    rows = length // GRID_W
    r = np.repeat(np.arange(rows, dtype=np.float64), GRID_W)
    col = np.tile(np.arange(GRID_W, dtype=np.float64), rows)
    nf = d // 4
    freqs = 1.0 / (10000.0 ** (np.arange(nf, dtype=np.float64) / nf))

    def emb(p):
        a = p[:, None] * freqs[None, :]
        return np.concatenate([np.sin(a), np.cos(a)], axis=-1)

    return np.concatenate([emb(r), emb(col)], axis=-1).astype(np.float32)


def _lb_kernel(raw_ref, out_ref):
    x = raw_ref[...]
    m = jnp.max(x, axis=0, keepdims=True)
    e = jnp.exp(x - m)
    p = e / jnp.sum(e, axis=0, keepdims=True)
    cs = p[0:1]
    first = cs
    out_ref[0:1, :] = cs - first
    for l in range(1, DEPTH):
        cs = cs + p[l:l + 1]
        out_ref[l:l + 1, :] = cs - first


def _lower_bounds(lb_raw):
    raw = lb_raw.reshape(DEPTH, 2 * D_A)
    out = pl.pallas_call(
        _lb_kernel,
        out_shape=jax.ShapeDtypeStruct((DEPTH, 2 * D_A), F32),
        name="lower_bounds",
    )(raw)
    return out.reshape(DEPTH * 2, 1, D_A)


def _mod_kernel(cond_ref, w_ref, b_ref, out_ref):
    a = _silu(cond_ref[...])
    out_ref[...] = jnp.dot(a, w_ref[...], preferred_element_type=F32,
                           precision=lax.Precision.HIGHEST) + b_ref[...]


def _modulation(cond, w_ada, b_ada):
    tn = 512
    out = pl.pallas_call(
        _mod_kernel,
        grid=(DEPTH, 3 * D // tn),
        in_specs=[
            pl.BlockSpec((MOD_ROWS, D), lambda l, j: (0, 0)),
            pl.BlockSpec((None, D, tn), lambda l, j: (l, 0, j)),
            pl.BlockSpec((None, 1, tn), lambda l, j: (l, 0, j)),
        ],
        out_specs=pl.BlockSpec((None, MOD_ROWS, tn), lambda l, j: (l, 0, j)),
        out_shape=jax.ShapeDtypeStruct((DEPTH, MOD_ROWS, 3 * D), F32),
        compiler_params=_cparams(2),
        name="modulation",
    )(cond, w_ada, b_ada.reshape(DEPTH, 1, 3 * D))
    return out.reshape(DEPTH * MOD_ROWS, 1, 3 * D)


_Q0, _FF0, _V0, _ZA0, _U0, _ZB0, _GA0, _GB0 = (
    0, D_A, 3 * D_A, 4 * D_A, 5 * D_A, 5 * D_A + D_B, 5 * D_A + 2 * D_B, 5 * D_A + 2 * D_B + D)
PA_Q, PA_V, PA_Z, PA_GA, PA_GB, PA_KF = 0, 1, 2, 3, 4, 5
PA_WIDTH = 7 * D


def _forget_gate(x, lb):
    f = lb + (1.0 - lb) * _sigmoid(x)
    log_f = jnp.where(f < 1e-30, jnp.minimum(x, 0.0), jnp.log(jnp.maximum(f, 1e-30)))
    return log_f, 1.0 - f


def _proj_kernel(has_pe, *refs):
    refs = list(refs)
    x = refs[0][...]
    pos = 1
    if has_pe:
        x = x + refs[pos][...]
        pos += 1
    mod_ref, nw_ref, lbf_ref, lbb_ref, w_ref, cc_ref, sc_ref = refs[pos:pos + 7]
    pf_ref, pa_ref, zr_ref, zi_ref, zb_ref = refs[pos + 7:]

    ms = jnp.mean(x * x, axis=-1, keepdims=True)
    y = x * lax.rsqrt(ms + EPS) * nw_ref[...]
    shift = mod_ref[:, 0:D]
    scale = mod_ref[:, D:2 * D]
    h = (y * (1.0 + scale) + shift).astype(BF16)

    def sec(c0, width):
        return _dot(h, w_ref[:, c0:c0 + width])

    def put(block, val):
        pa_ref[:, block * D:(block + 1) * D] = val.astype(BF16)

    tm = x.shape[0]
    row = lax.broadcasted_iota(jnp.int32, (tm, tm), 0)
    col = lax.broadcasted_iota(jnp.int32, (tm, tm), 1)
    same_chunk = (row // SCAN_CHUNK) == (col // SCAN_CHUNK)
    tris = (jnp.where(same_chunk & (col <= row), 1.0, 0.0).astype(BF16),
            jnp.where(same_chunk & (col >= row), 1.0, 0.0).astype(BF16))
    for d, lb_ref in enumerate((lbf_ref, lbb_ref)):
        g, kk = _forget_gate(sec(_FF0 + d * D_A, D_A), lb_ref[...])
        g_hi = g.astype(BF16)
        g_lo = (g - g_hi.astype(F32)).astype(BF16)
        pf_ref[:, d * D_A:(d + 1) * D_A] = _dot(tris[d], g_hi) + _dot(tris[d], g_lo)
        put(PA_KF + d, kk)

    put(PA_Q, _silu(sec(_Q0, D_A)))
    put(PA_V, sec(_V0, D_A))
    put(PA_Z, _silu(sec(_ZA0, D_A)))
    put(PA_GA, _sigmoid(sec(_GA0, D)))
    put(PA_GB, _sigmoid(sec(_GB0, D)))
    zb_ref[...] = _silu(sec(_ZB0, D_B)).astype(BF16)
    u = sec(_U0, D_B).astype(BF16)
    cc = cc_ref[...].astype(BF16)
    sc = sc_ref[...].astype(BF16)
    for g in range(N_GROUPS):
        ug = u[:, g * GROUP:(g + 1) * GROUP]
        zr_ref[:, g * GROUP:(g + 1) * GROUP] = _dot(ug, cc).astype(BF16)
        zi_ref[:, g * GROUP:(g + 1) * GROUP] = (-_dot(ug, sc)).astype(BF16)


def _proj(x, pe, mod, norm_w3, lbs, w_in_bf, cc, sc, layer, seq_len, latent):
    tokens = x.shape[0]
    tm = TM_PROJ
    has_pe = pe is not None
    per_seq = seq_len // tm
    if latent:
        mod_idx = lambda i: (layer * MOD_ROWS + 1 + i // per_seq, 0, 0)
    else:
        mod_idx = lambda i: (layer * MOD_ROWS, 0, 0)
    in_specs = [pl.BlockSpec((tm, D), lambda i: (i, 0))]
    args = [x]
    if has_pe:
        in_specs.append(pl.BlockSpec((tm, D), lambda i: (i % per_seq, 0)))
        args.append(pe)
    in_specs += [
        pl.BlockSpec((None, 1, 3 * D), mod_idx),
        pl.BlockSpec((None, 1, D), lambda i: (layer, 0, 0)),
        pl.BlockSpec((None, 1, D_A), lambda i: (2 * layer, 0, 0)),
        pl.BlockSpec((None, 1, D_A), lambda i: (2 * layer + 1, 0, 0)),
        pl.BlockSpec((None, D, D_IN), lambda i: (layer, 0, 0), pipeline_mode=pl.Buffered(1)),
        pl.BlockSpec((GROUP, GROUP), lambda i: (0, 0)),
        pl.BlockSpec((GROUP, GROUP), lambda i: (0, 0)),
    ]
    args += [mod, norm_w3, lbs, lbs, w_in_bf, cc, sc]
    out_shape = (
        jax.ShapeDtypeStruct((tokens, 2 * D_A), F32),
        jax.ShapeDtypeStruct((tokens, PA_WIDTH), BF16),
        jax.ShapeDtypeStruct((tokens, D_B), BF16),
        jax.ShapeDtypeStruct((tokens, D_B), BF16),
        jax.ShapeDtypeStruct((tokens, D_B), BF16),
    )
    out_specs = (
        pl.BlockSpec((tm, 2 * D_A), lambda i: (i, 0)),
        pl.BlockSpec((tm, PA_WIDTH), lambda i: (i, 0)),
        pl.BlockSpec((tm, D_B), lambda i: (i, 0)),
        pl.BlockSpec((tm, D_B), lambda i: (i, 0)),
        pl.BlockSpec((tm, D_B), lambda i: (i, 0)),
    )
    return pl.pallas_call(
        functools.partial(_proj_kernel, has_pe),
        grid=(tokens // tm,),
        in_specs=in_specs,
        out_specs=out_specs,
        out_shape=out_shape,
        compiler_params=_cparams(1),
        name="proj",
    )(*args)


def _scan_kernel(seq_len, has_init, emit_state, *refs):
    refs = list(refs)
    bf_ref, bb_ref, q_ref, v_ref, z_ref, kf_ref, kb_ref, gn_ref = refs[:8]
    pos = 8
    init_ref = None
    if has_init:
        init_ref = refs[pos]
        pos += 1
    y_ref = refs[pos]
    pos += 1
    st_ref = None
    if emit_state:
        st_ref = refs[pos]
        pos += 1
    (o_ref, sf_ref, sb_ref, qinf_ref, qinb_ref, uf_ref, ub_ref, elf_ref, elb_ref) = refs[pos:pos + 9]

    C = SCAN_CHUNK
    H = C // 2
    n_chunks = seq_len // C
    G = min(SCAN_GROUP, n_chunks)
    GC = G * C

    def pair_masks(n):
        rr = lax.broadcasted_iota(jnp.int32, (n, n), 0)
        cc = lax.broadcasted_iota(jnp.int32, (n, n), 1)
        return cc <= rr, cc >= rr

    mask_c = pair_masks(C)
    mask_h = pair_masks(H)
    dirs = (
        (bf_ref, kf_ref, C - 1, qinf_ref, uf_ref, elf_ref),
        (bb_ref, kb_ref, 0, qinb_ref, ub_ref, elb_ref),
    )

    def split_factors(bq, qv, kv):
        n = bq.shape[0]
        r = 0.5 * (bq[0:1, :] + bq[n - 1:n, :])
        e = jnp.exp(jnp.clip(bq - r, -EXP_CLAMP, EXP_CLAMP))
        return qv * e, kv * (1.0 / e), r, jnp.abs(bq[0:1, :] - r)

    def halves(d, bc, qc, kc, vc):
        first, second = (slice(0, H), slice(H, C)) if d == 0 else (slice(H, C), slice(0, H))
        edge = bc[H - 1:H, :] if d == 0 else bc[H:H + 1, :]
        atts, dev = [], None
        for rows in (first, second):
            qt, kt, _, dev_h = split_factors(bc[rows], qc[rows], kc[rows])
            att = _dot_nt(qt.astype(BF16), kt.astype(BF16))
            atts.append(jnp.where(mask_h[d], att, 0.0).astype(BF16))
            dev = dev_h if dev is None else jnp.maximum(dev, dev_h)
        cross = _dot_nt((qc[second] * jnp.exp(bc[second] - edge)).astype(BF16),
                        (kc[first] * jnp.exp(edge - bc[first])).astype(BF16)).astype(BF16)
        o_first = _dot(atts[0], vc[first])
        o_second = _dot(atts[1], vc[second]) + _dot(cross, vc[first])
        parts = [o_first, o_second] if d == 0 else [o_second, o_first]
        return jnp.concatenate(parts, axis=0), dev

    def pairwise(d, bc, qc, kc, vf):
        s_idx = lax.broadcasted_iota(jnp.int32, (C, 1), 0)

        def row(t, o_blk):
            sel = s_idx == t
            b_t = jnp.sum(jnp.where(sel, bc, 0.0), axis=0, keepdims=True)
            q_t = jnp.sum(jnp.where(sel, qc, 0.0), axis=0, keepdims=True)
            allowed = (s_idx <= t) if d == 0 else (s_idx >= t)
            decay = jnp.where(allowed, jnp.exp(jnp.where(allowed, b_t - bc, 0.0)), 0.0)
            att_t = jnp.sum(decay * (q_t * kc), axis=1, keepdims=True)
            o_t = jnp.sum(att_t * vf, axis=0, keepdims=True)
            return jnp.where(sel, o_t, o_blk)

        return lax.fori_loop(0, C, row, jnp.zeros((C, HEAD), F32))

    def redo_chunk(chunk, rows):
        bcs = [b_ref[rows, :] for b_ref, _, _, _, _, _ in dirs]
        span = None
        for bc in bcs:
            r = 0.5 * (bc[0:1, :] + bc[C - 1:C, :])
            dev = jnp.maximum(jnp.abs(bc[0:1, :] - r), jnp.abs(r))
            span = dev if span is None else jnp.maximum(span, dev)

        @pl.when(jnp.max(span) > EXP_CLAMP)
        def _():
            qc = q_ref[rows, :].astype(F32)
            vc = v_ref[rows, :]
            kcs = [k_ref[rows, :].astype(F32) for _, k_ref, _, _, _, _ in dirs]
            o_sum, span_h = None, None
            for d, (_, _, last_row, qin_ref, u_ref, _) in enumerate(dirs):
                bc, kc = bcs[d], kcs[d]
                b_last = bc[last_row:last_row + 1, :]
                qin_ref[rows, :] = (qc * jnp.exp(bc)).astype(BF16)
                u_ref[chunk] = _dot_tn(vc, (kc * jnp.exp(b_last - bc)).astype(BF16))
                o_d, dev = halves(d, bc, qc, kc, vc)
                o_sum = o_d if o_sum is None else o_sum + o_d
                span_h = dev if span_h is None else jnp.maximum(span_h, dev)
            o_ref[rows, :] = o_sum

            @pl.when(jnp.max(span_h) > EXP_CLAMP)
            def _():
                vf = vc.astype(F32)
                o_ref[rows, :] = pairwise(0, bcs[0], qc, kcs[0], vf) + pairwise(1, bcs[1], qc, kcs[1], vf)

    def local(j, carry):
        r0 = pl.multiple_of(j * GC, GC)
        q = q_ref[pl.ds(r0, GC), :].astype(F32)
        v = v_ref[pl.ds(r0, GC), :]
        work = []
        span = None
        for d, (b_ref, k_ref, last_row, qin_ref, u_ref, el_ref) in enumerate(dirs):
            b = b_ref[pl.ds(r0, GC), :]
            kk = k_ref[pl.ds(r0, GC), :].astype(F32)
            for c in range(G):
                sl = slice(c * C, (c + 1) * C)
                bc = b[sl]
                b_last = bc[last_row:last_row + 1, :]
                qt, kt, r, dev = split_factors(bc, q[sl], kk[sl])
                dev = jnp.maximum(dev, jnp.abs(r))
                span = dev if span is None else jnp.maximum(span, dev)
                qin_ref[pl.ds(r0 + c * C, C), :] = (qt * jnp.exp(r)).astype(BF16)
                el_ref[j * G + c] = jnp.exp(b_last)
                k_out = (kt * jnp.exp(b_last - r)).astype(BF16)
                work.append((qt.astype(BF16), kt.astype(BF16), k_out, mask_c[d], v[sl], u_ref, c))
        atts = [_dot_nt(qt, kt) for qt, kt, _, _, _, _, _ in work]
        atts = [jnp.where(w[3], a, 0.0).astype(BF16) for w, a in zip(work, atts)]
        for (_, _, k_out, _, vc, u_ref, c) in work:
            u_ref[j * G + c] = _dot_tn(vc, k_out)
        outs = [_dot(a, w[4]) for w, a in zip(work, atts)]
        o_f = jnp.concatenate(outs[:G], axis=0)
        o_b = jnp.concatenate(outs[G:], axis=0)
        o_ref[pl.ds(r0, GC), :] = o_f + o_b

        @pl.when(jnp.max(span) > EXP_CLAMP)
        def _():
            def redo(c, carry):
                redo_chunk(j * G + c, pl.ds(pl.multiple_of(j * GC + c * C, C), C))
                return carry

            lax.fori_loop(0, G, redo, 0)

        return carry

    lax.fori_loop(0, seq_len // GC, local, 0)

    if has_init:
        s_init = (init_ref[0].T, init_ref[1].T)
    else:
        s_init = (jnp.zeros((HEAD, HEAD), F32), jnp.zeros((HEAD, HEAD), F32))

    def step(i, carry):
        s_f, s_b = carry
        jb = n_chunks - 1 - i
        sf_ref[i] = s_f.astype(BF16)
        sb_ref[jb] = s_b.astype(BF16)
        return (s_f * elf_ref[i] + uf_ref[i], s_b * elb_ref[jb] + ub_ref[jb])

    s_f, s_b = lax.fori_loop(0, n_chunks, step, s_init, unroll=SCAN_UNROLL)

    gn = gn_ref[...]

    def add_state_readout(j):
        r0 = pl.multiple_of(j * GC, GC)
        o = o_ref[pl.ds(r0, GC), :]
        for qin_ref, s_ref in ((qinf_ref, sf_ref), (qinb_ref, sb_ref)):
            o = o + jnp.concatenate(
                [_dot_nt(qin_ref[pl.ds(r0 + c * C, C), :], s_ref[j * G + c]) for c in range(G)], axis=0)
        o_ref[pl.ds(r0, GC), :] = o

    def normalise(j):
        r0 = pl.multiple_of(j * GC, GC)
        o = o_ref[pl.ds(r0, GC), :]
        ms = jnp.mean(o * o, axis=-1, keepdims=True)
        o = o * lax.rsqrt(ms + EPS) * gn
        y_ref[pl.ds(r0, GC), :] = (o * z_ref[pl.ds(r0, GC), :].astype(F32)).astype(BF16)

    n_groups = seq_len // GC
    add_state_readout(0)

    def finish(j, carry):
        normalise(j - 1)
        add_state_readout(j)
        return carry

    lax.fori_loop(1, n_groups, finish, 0)
    normalise(n_groups - 1)

    if emit_state:
        st_ref[0] = s_f.T
        st_ref[1] = s_b.T


def _scan(pf, pa, gnorm3, init_state, layer, n_seq, seq_len, emit_state):
    tokens = n_seq * seq_len
    n_chunks = seq_len // SCAN_CHUNK
    has_init = init_state is not None
    blk = lambda c0: pl.BlockSpec((seq_len, HEAD), lambda b, h, c0=c0: (b, c0 + h))
    in_specs = [
        blk(0), blk(N_HEADS),
        blk(PA_Q * N_HEADS), blk(PA_V * N_HEADS), blk(PA_Z * N_HEADS),
        blk(PA_KF * N_HEADS), blk((PA_KF + 1) * N_HEADS),
        pl.BlockSpec((None, 1, HEAD), lambda b, h: (layer, 0, h)),
    ]
    args = [pf, pf, pa, pa, pa, pa, pa, gnorm3]
    if has_init:
        in_specs.append(pl.BlockSpec((None, None, 2, None, HEAD, HEAD),
                                     lambda b, h: (b, layer, 0, h, 0, 0)))
        args.append(init_state)
    out_shape = [jax.ShapeDtypeStruct((tokens, D_A), BF16)]
    out_specs = [pl.BlockSpec((seq_len, HEAD), lambda b, h: (b, h))]
    if emit_state:
        out_shape.append(jax.ShapeDtypeStruct((n_seq, 2, N_HEADS, HEAD, HEAD), F32))
        out_specs.append(pl.BlockSpec((None, 2, None, HEAD, HEAD), lambda b, h: (b, 0, h, 0, 0)))
    res = pl.pallas_call(
        functools.partial(_scan_kernel, seq_len, has_init, emit_state),
        grid=(n_seq, N_HEADS),
        in_specs=in_specs,
        out_specs=out_specs,
        out_shape=out_shape,
        scratch_shapes=[
            pltpu.VMEM((seq_len, HEAD), F32),
            pltpu.VMEM((n_chunks, HEAD, HEAD), BF16), pltpu.VMEM((n_chunks, HEAD, HEAD), BF16),
            pltpu.VMEM((seq_len, HEAD), BF16), pltpu.VMEM((seq_len, HEAD), BF16),
            pltpu.VMEM((n_chunks, HEAD, HEAD), F32), pltpu.VMEM((n_chunks, HEAD, HEAD), F32),
            pltpu.VMEM((n_chunks, 1, HEAD), F32), pltpu.VMEM((n_chunks, 1, HEAD), F32),
        ],
        compiler_params=_cparams(2),
        name="scan",
    )(*args)
    return res if emit_state else (res[0], None)


def _fourier_dense_kernel(scale, zr_ref, zi_ref, zb_ref, c_ref, s_ref, y_ref):
    yf = (_dot(c_ref[...].astype(BF16), zr_ref[...])
          + _dot(s_ref[...].astype(BF16), zi_ref[...]))
    y_ref[...] = (yf * scale * zb_ref[...].astype(F32)).astype(BF16)


def _fourier_dense(zr, zi, zb, n_seq, seq_len):
    c, s = _dft_cos_sin(seq_len)
    scale = 1.0 / np.sqrt(seq_len * GROUP)
    blk = pl.BlockSpec((seq_len, D_B), lambda b: (b, 0))
    mat = pl.BlockSpec((seq_len, seq_len), lambda b: (0, 0))
    return pl.pallas_call(
        functools.partial(_fourier_dense_kernel, scale),
        grid=(n_seq,),
        in_specs=[blk, blk, blk, mat, mat],
        out_specs=blk,
        out_shape=jax.ShapeDtypeStruct((n_seq * seq_len, D_B), BF16),
        compiler_params=_cparams(1),
        name="fourier_dense",
    )(zr, zi, zb, jnp.asarray(c, F32), jnp.asarray(s, F32))


def _fourier_stage1_kernel(zr_ref, zi_ref, m_ref, tc_ref, ts_ref, br_ref, bi_ref):
    z = jnp.concatenate([zr_ref[...], zi_ref[...]], axis=0)
    a = _dot(m_ref[...].astype(BF16), z)
    ar = a[:FFT_RADIX]
    ai = a[FFT_RADIX:]
    tc = tc_ref[...]
    ts = ts_ref[...]
    br_ref[...] = (ar * tc + ai * ts).astype(BF16)
    bi_ref[...] = (ai * tc - ar * ts).astype(BF16)


def _fourier_stage2_kernel(scale, br_ref, bi_ref, zb_ref, cs_ref, y_ref):
    cs = cs_ref[...].astype(BF16)
    for j in range(FFT_BLOCK):
        bcat = jnp.concatenate([br_ref[j], bi_ref[j]], axis=0)
        yf = _dot(cs, bcat) * scale
        gate = zb_ref[:, j * D_B:(j + 1) * D_B].astype(F32)
        y_ref[:, j * D_B:(j + 1) * D_B] = (yf * gate).astype(BF16)


def _fourier_two_stage(zr, zi, zb, n_seq, seq_len):
    R = FFT_RADIX
    assert seq_len == R * R
    wide = R * D_B
    cols = FFT_BLOCK * D_B
    c, s = _dft_cos_sin(R)
    m1 = np.block([[c, s], [-s, c]])
    k1 = np.arange(R, dtype=np.float64)
    ang = 2.0 * np.pi * np.outer(k1, k1) / seq_len
    tc = np.repeat(np.cos(ang), D_B, axis=1).astype(np.float32)
    ts = np.repeat(np.sin(ang), D_B, axis=1).astype(np.float32)
    cs = np.concatenate([c, s], axis=1)
    scale = 1.0 / np.sqrt(seq_len * GROUP)

    zr3 = zr.reshape(n_seq, R, wide)
    zi3 = zi.reshape(n_seq, R, wide)
    seq_blk = pl.BlockSpec((None, R, cols), lambda b, j: (b, 0, j))
    br, bi = pl.pallas_call(
        _fourier_stage1_kernel,
        grid=(n_seq, R // FFT_BLOCK),
        in_specs=[seq_blk, seq_blk,
                  pl.BlockSpec((2 * R, 2 * R), lambda b, j: (0, 0)),
                  pl.BlockSpec((R, cols), lambda b, j: (0, j)),
                  pl.BlockSpec((R, cols), lambda b, j: (0, j))],
        out_specs=(seq_blk, seq_blk),
        out_shape=(jax.ShapeDtypeStruct((n_seq, R, wide), BF16),) * 2,
        compiler_params=_cparams(2),
        name="fourier_stage1",
    )(zr3, zi3, jnp.asarray(m1, F32), jnp.asarray(tc), jnp.asarray(ts))

    br4 = br.reshape(n_seq, R, R, D_B)
    bi4 = bi.reshape(n_seq, R, R, D_B)
    zb3 = zb.reshape(n_seq, R, wide)
    k_blk = pl.BlockSpec((None, FFT_BLOCK, R, D_B), lambda b, j: (b, j, 0, 0))
    y = pl.pallas_call(
        functools.partial(_fourier_stage2_kernel, scale),
        grid=(n_seq, R // FFT_BLOCK),
        in_specs=[k_blk, k_blk, seq_blk, pl.BlockSpec((R, 2 * R), lambda b, j: (0, 0))],
        out_specs=seq_blk,
        out_shape=jax.ShapeDtypeStruct((n_seq, R, wide), BF16),
        compiler_params=_cparams(2),
        name="fourier_stage2",
    )(br4, bi4, zb3, jnp.asarray(cs, F32))
    return y.reshape(n_seq * seq_len, D_B)


def _merge_kernel(has_pe, final, *refs):
    refs = list(refs)
    ya_ref, yb_ref, ga_ref, gb_ref, x_ref = refs[:5]
    pos = 5
    x = x_ref[...]
    if has_pe:
        x = x + refs[pos][...]
        pos += 1
    mod_ref, wpa_ref, wpb_ref, wo_ref = refs[pos:pos + 4]
    pos += 4
    if final:
        fw_ref = refs[pos]
        pos += 1
    out_ref = refs[pos]
    merged = (ga_ref[...].astype(F32) * _dot(ya_ref[...], wpa_ref[...])
              + gb_ref[...].astype(F32) * _dot(yb_ref[...], wpb_ref[...]))
    out = _dot(merged.astype(BF16), wo_ref[...])
    xn = x + mod_ref[:, 2 * D:3 * D] * out
    if final:
        ms = jnp.mean(xn * xn, axis=-1, keepdims=True)
        xn = xn * lax.rsqrt(ms + EPS) * fw_ref[...]
    out_ref[...] = xn


def _merge(ya, yb, pa, x, pe, mod, wpa, wpb, wo, final_w, layer, seq_len, latent):
    tokens = x.shape[0]
    tm = TM_MERGE
    has_pe = pe is not None
    final = final_w is not None
    row = lambda i: (i, 0)
    if latent:
        mod_idx = lambda i: (layer * MOD_ROWS + 1 + (i * tm) // seq_len, 0, 0)
    else:
        mod_idx = lambda i: (layer * MOD_ROWS, 0, 0)
    in_specs = [
        pl.BlockSpec((tm, D_A), row),
        pl.BlockSpec((tm, D_B), row),
        pl.BlockSpec((tm, D), lambda i: (i, PA_GA)),
        pl.BlockSpec((tm, D), lambda i: (i, PA_GB)),
        pl.BlockSpec((tm, D), row),
    ]
    args = [ya, yb, pa, pa, x]
    if has_pe:
        per_seq = seq_len // tm
        in_specs.append(pl.BlockSpec((tm, D), lambda i: (i % per_seq, 0)))
        args.append(pe)
    in_specs += [
        pl.BlockSpec((None, 1, 3 * D), mod_idx),
        pl.BlockSpec((None, D_A, D), lambda i: (layer, 0, 0)),
        pl.BlockSpec((None, D_B, D), lambda i: (layer, 0, 0)),
        pl.BlockSpec((None, D, D), lambda i: (layer, 0, 0)),
    ]
    args += [mod, wpa, wpb, wo]
    if final:
        in_specs.append(pl.BlockSpec((1, D), lambda i: (0, 0)))
        args.append(final_w)
    return pl.pallas_call(
        functools.partial(_merge_kernel, has_pe, final),
        grid=(tokens // tm,),
        in_specs=in_specs,
        out_specs=pl.BlockSpec((tm, D), row),
        out_shape=jax.ShapeDtypeStruct((tokens, D), F32),
        compiler_params=_cparams(1),
        name="merge",
    )(*args)


def kernel(x_prompt, x_sample, state_hgrn, c, c_ctx, norm_w, w_ada, b_ada, w_in, lb_raw,
           gnorm_w, w_pa, w_pb, w_o, final_norm_w):
    n_p, len_p, _ = x_prompt.shape
    n_s, len_s, _ = x_sample.shape

    cond = jnp.zeros((MOD_ROWS, D), F32).at[0].set(c_ctx).at[1:1 + n_s].set(c)
    mod = _modulation(cond, w_ada, b_ada)
    lbs = _lower_bounds(lb_raw)

    w_in_bf = w_in.astype(BF16)
    wpa_bf = w_pa.astype(BF16)
    wpb_bf = w_pb.astype(BF16)
    wo_bf = w_o.astype(BF16)
    norm_w3 = norm_w.reshape(DEPTH, 1, D)
    gnorm3 = gnorm_w.reshape(DEPTH, 1, D_A)
    final_w = final_norm_w.reshape(1, D)

    cc_np, sc_np = _dft_cos_sin(GROUP)
    cc = jnp.asarray(cc_np, F32)
    sc = jnp.asarray(sc_np, F32)
    pe = jnp.asarray(_pos_embed_table(len_s, D))

    xp = x_prompt.reshape(n_p * len_p, D)
    xs = x_sample.reshape(n_s * len_s, D)
    states = []
    for l in range(DEPTH):
        last = l == DEPTH - 1
        fw = final_w if last else None
        pe_l = pe if l == 0 else None

        pf, pa, zr, zi, zb = _proj(xp, None, mod, norm_w3, lbs, w_in_bf, cc, sc, l, len_p, False)
        ya, st = _scan(pf, pa, gnorm3, None, l, n_p, len_p, True)
        yb = _fourier_dense(zr, zi, zb, n_p, len_p)
        xp = _merge(ya, yb, pa, xp, None, mod, wpa_bf, wpb_bf, wo_bf, fw, l, len_p, False)
        states.append(st)

        pf, pa, zr, zi, zb = _proj(xs, pe_l, mod, norm_w3, lbs, w_in_bf, cc, sc, l, len_s, True)
        ya, _ = _scan(pf, pa, gnorm3, state_hgrn, l, n_s, len_s, False)
        yb = _fourier_two_stage(zr, zi, zb, n_s, len_s)
        xs = _merge(ya, yb, pa, xs, pe_l, mod, wpa_bf, wpb_bf, wo_bf, fw, l, len_s, True)

    y_prompt = xp.reshape(n_p, len_p, D)
    y_sample = xs.reshape(n_s, len_s, D)
    new_state = jnp.stack(states, axis=1)
    return (y_prompt, y_sample, new_state)
```

```python
import functools

import jax
import jax.numpy as jnp
import numpy as np
from jax import lax
from jax.experimental import pallas as pl
from jax.experimental.pallas import tpu as pltpu

D = 1024
DEPTH = 4
N_HEADS = 8
HEAD = 128
D_A = N_HEADS * HEAD
N_GROUPS = 4
GROUP = 128
D_B = N_GROUPS * GROUP
D_IN = 5 * D_A + 2 * D_B + 2 * D
GRID_W = 64
EPS = 1e-6

V7X_VMEM_LIMIT_BYTES = 56 * 1024 * 1024

SCAN_CHUNK = 64
SCAN_GROUP = 8
SCAN_UNROLL = 4
EXP_CLAMP = 80.0
TM_PROJ = 256
TM_MERGE = 512
FFT_RADIX = 64
FFT_BLOCK = 8
MOD_ROWS = 8

F32 = jnp.float32
BF16 = jnp.bfloat16


def _cparams(n_axes):
    return pltpu.CompilerParams(
        dimension_semantics=("arbitrary",) * n_axes,
        vmem_limit_bytes=V7X_VMEM_LIMIT_BYTES,
    )


def _dot(a, b):
    return jnp.dot(a, b, preferred_element_type=F32)


def _dot_nt(a, b):
    return lax.dot_general(a, b, (((1,), (1,)), ((), ())), preferred_element_type=F32)


def _dot_tn(a, b):
    return lax.dot_general(a, b, (((0,), (0,)), ((), ())), preferred_element_type=F32)


def _sigmoid(x):
    return 1.0 / (1.0 + jnp.exp(-x))


def _silu(x):
    return x * _sigmoid(x)


def _dft_cos_sin(n):
    k = np.arange(n, dtype=np.float64)
    ang = 2.0 * np.pi * np.outer(k, k) / n
    return np.cos(ang), np.sin(ang)


def _pos_embed_table(length, d):
    rows = length // GRID_W
    r = np.repeat(np.arange(rows, dtype=np.float64), GRID_W)
    col = np.tile(np.arange(GRID_W, dtype=np.float64), rows)
    nf = d // 4
    freqs = 1.0 / (10000.0 ** (np.arange(nf, dtype=np.float64) / nf))

    def emb(p):
        a = p[:, None] * freqs[None, :]
        return np.concatenate([np.sin(a), np.cos(a)], axis=-1)

    return np.concatenate([emb(r), emb(col)], axis=-1).astype(np.float32)


def _lb_kernel(raw_ref, out_ref):
    x = raw_ref[...]
    m = jnp.max(x, axis=0, keepdims=True)
    e = jnp.exp(x - m)
    p = e / jnp.sum(e, axis=0, keepdims=True)
    cs = p[0:1]
    first = cs
    out_ref[0:1, :] = cs - first
    for l in range(1, DEPTH):
        cs = cs + p[l:l + 1]
        out_ref[l:l + 1, :] = cs - first


def _lower_bounds(lb_raw):
    raw = lb_raw.reshape(DEPTH, 2 * D_A)
    out = pl.pallas_call(
        _lb_kernel,
        out_shape=jax.ShapeDtypeStruct((DEPTH, 2 * D_A), F32),
        name="lower_bounds",
    )(raw)
    return out.reshape(DEPTH * 2, 1, D_A)


def _mod_kernel(cond_ref, w_ref, b_ref, out_ref):
    a = _silu(cond_ref[...])
    out_ref[...] = jnp.dot(a, w_ref[...], preferred_element_type=F32,
                           precision=lax.Precision.HIGHEST) + b_ref[...]


def _modulation(cond, w_ada, b_ada):
    tn = 512
    out = pl.pallas_call(
        _mod_kernel,
        grid=(DEPTH, 3 * D // tn),
        in_specs=[
            pl.BlockSpec((MOD_ROWS, D), lambda l, j: (0, 0)),
            pl.BlockSpec((None, D, tn), lambda l, j: (l, 0, j)),
            pl.BlockSpec((None, 1, tn), lambda l, j: (l, 0, j)),
        ],
        out_specs=pl.BlockSpec((None, MOD_ROWS, tn), lambda l, j: (l, 0, j)),
        out_shape=jax.ShapeDtypeStruct((DEPTH, MOD_ROWS, 3 * D), F32),
        compiler_params=_cparams(2),
        name="modulation",
    )(cond, w_ada, b_ada.reshape(DEPTH, 1, 3 * D))
    return out.reshape(DEPTH * MOD_ROWS, 1, 3 * D)


_Q0, _FF0, _V0, _ZA0, _U0, _ZB0, _GA0, _GB0 = (
    0, D_A, 3 * D_A, 4 * D_A, 5 * D_A, 5 * D_A + D_B, 5 * D_A + 2 * D_B, 5 * D_A + 2 * D_B + D)
PA_Q, PA_V, PA_Z, PA_GA, PA_GB, PA_KF = 0, 1, 2, 3, 4, 5
PA_WIDTH = 7 * D


def _forget_gate(x, lb):
    f = lb + (1.0 - lb) * _sigmoid(x)
    log_f = jnp.where(f < 1e-30, jnp.minimum(x, 0.0), jnp.log(jnp.maximum(f, 1e-30)))
    return log_f, 1.0 - f


def _proj_kernel(has_pe, *refs):
    refs = list(refs)
    x = refs[0][...]
    pos = 1
    if has_pe:
        x = x + refs[pos][...]
        pos += 1
    mod_ref, nw_ref, lbf_ref, lbb_ref, w_ref, cc_ref, sc_ref = refs[pos:pos + 7]
    pf_ref, pa_ref, zr_ref, zi_ref, zb_ref = refs[pos + 7:]

    ms = jnp.mean(x * x, axis=-1, keepdims=True)
    y = x * lax.rsqrt(ms + EPS) * nw_ref[...]
    shift = mod_ref[:, 0:D]
    scale = mod_ref[:, D:2 * D]
    h = (y * (1.0 + scale) + shift).astype(BF16)

    def sec(c0, width):
        return _dot(h, w_ref[:, c0:c0 + width])

    def put(block, val):
        pa_ref[:, block * D:(block + 1) * D] = val.astype(BF16)

    tm = x.shape[0]
    row = lax.broadcasted_iota(jnp.int32, (tm, tm), 0)
    col = lax.broadcasted_iota(jnp.int32, (tm, tm), 1)
    same_chunk = (row // SCAN_CHUNK) == (col // SCAN_CHUNK)
    tris = (jnp.where(same_chunk & (col <= row), 1.0, 0.0).astype(BF16),
            jnp.where(same_chunk & (col >= row), 1.0, 0.0).astype(BF16))
    for d, lb_ref in enumerate((lbf_ref, lbb_ref)):
        g, kk = _forget_gate(sec(_FF0 + d * D_A, D_A), lb_ref[...])
        g_hi = g.astype(BF16)
        g_lo = (g - g_hi.astype(F32)).astype(BF16)
        pf_ref[:, d * D_A:(d + 1) * D_A] = _dot(tris[d], g_hi) + _dot(tris[d], g_lo)
        put(PA_KF + d, kk)

    put(PA_Q, _silu(sec(_Q0, D_A)))
    put(PA_V, sec(_V0, D_A))
    put(PA_Z, _silu(sec(_ZA0, D_A)))
    put(PA_GA, _sigmoid(sec(_GA0, D)))
    put(PA_GB, _sigmoid(sec(_GB0, D)))
    zb_ref[...] = _silu(sec(_ZB0, D_B)).astype(BF16)
    u = sec(_U0, D_B).astype(BF16)
    cc = cc_ref[...].astype(BF16)
    sc = sc_ref[...].astype(BF16)
    for g in range(N_GROUPS):
        ug = u[:, g * GROUP:(g + 1) * GROUP]
        zr_ref[:, g * GROUP:(g + 1) * GROUP] = _dot(ug, cc)
        zi_ref[:, g * GROUP:(g + 1) * GROUP] = -_dot(ug, sc)


def _proj(x, pe, mod, norm_w3, lbs, w_in_bf, cc, sc, layer, seq_len, latent):
    tokens = x.shape[0]
    tm = TM_PROJ
    has_pe = pe is not None
    per_seq = seq_len // tm
    if latent:
        mod_idx = lambda i: (layer * MOD_ROWS + 1 + i // per_seq, 0, 0)
    else:
        mod_idx = lambda i: (layer * MOD_ROWS, 0, 0)
    in_specs = [pl.BlockSpec((tm, D), lambda i: (i, 0))]
    args = [x]
    if has_pe:
        in_specs.append(pl.BlockSpec((tm, D), lambda i: (i % per_seq, 0)))
        args.append(pe)
    in_specs += [
        pl.BlockSpec((None, 1, 3 * D), mod_idx),
        pl.BlockSpec((None, 1, D), lambda i: (layer, 0, 0)),
        pl.BlockSpec((None, 1, D_A), lambda i: (2 * layer, 0, 0)),
        pl.BlockSpec((None, 1, D_A), lambda i: (2 * layer + 1, 0, 0)),
        pl.BlockSpec((None, D, D_IN), lambda i: (layer, 0, 0), pipeline_mode=pl.Buffered(1)),
        pl.BlockSpec((GROUP, GROUP), lambda i: (0, 0)),
        pl.BlockSpec((GROUP, GROUP), lambda i: (0, 0)),
    ]
    args += [mod, norm_w3, lbs, lbs, w_in_bf, cc, sc]
    out_shape = (
        jax.ShapeDtypeStruct((tokens, 2 * D_A), F32),
        jax.ShapeDtypeStruct((tokens, PA_WIDTH), BF16),
        jax.ShapeDtypeStruct((tokens, D_B), F32),
        jax.ShapeDtypeStruct((tokens, D_B), F32),
        jax.ShapeDtypeStruct((tokens, D_B), BF16),
    )
    out_specs = (
        pl.BlockSpec((tm, 2 * D_A), lambda i: (i, 0)),
        pl.BlockSpec((tm, PA_WIDTH), lambda i: (i, 0)),
        pl.BlockSpec((tm, D_B), lambda i: (i, 0)),
        pl.BlockSpec((tm, D_B), lambda i: (i, 0)),
        pl.BlockSpec((tm, D_B), lambda i: (i, 0)),
    )
    return pl.pallas_call(
        functools.partial(_proj_kernel, has_pe),
        grid=(tokens // tm,),
        in_specs=in_specs,
        out_specs=out_specs,
        out_shape=out_shape,
        compiler_params=_cparams(1),
        name="proj",
    )(*args)


def _scan_kernel(seq_len, has_init, emit_state, *refs):
    refs = list(refs)
    bf_ref, bb_ref, q_ref, v_ref, z_ref, kf_ref, kb_ref, gn_ref = refs[:8]
    pos = 8
    init_ref = None
    if has_init:
        init_ref = refs[pos]
        pos += 1
    y_ref = refs[pos]
    pos += 1
    st_ref = None
    if emit_state:
        st_ref = refs[pos]
        pos += 1
    (o_ref, sf_ref, sb_ref, qinf_ref, qinb_ref, uf_ref, ub_ref, elf_ref, elb_ref) = refs[pos:pos + 9]

    C = SCAN_CHUNK
    H = C // 2
    n_chunks = seq_len // C
    G = min(SCAN_GROUP, n_chunks)
    GC = G * C

    def pair_masks(n):
        rr = lax.broadcasted_iota(jnp.int32, (n, n), 0)
        cc = lax.broadcasted_iota(jnp.int32, (n, n), 1)
        return cc <= rr, cc >= rr

    mask_c = pair_masks(C)
    mask_h = pair_masks(H)
    dirs = (
        (bf_ref, kf_ref, C - 1, qinf_ref, uf_ref, elf_ref),
        (bb_ref, kb_ref, 0, qinb_ref, ub_ref, elb_ref),
    )

    def split_factors(bq, qv, kv):
        n = bq.shape[0]
        r = 0.5 * (bq[0:1, :] + bq[n - 1:n, :])
        e = jnp.exp(jnp.clip(bq - r, -EXP_CLAMP, EXP_CLAMP))
        return qv * e, kv * (1.0 / e), r, jnp.abs(bq[0:1, :] - r)

    def halves(d, bc, qc, kc, vc):
        first, second = (slice(0, H), slice(H, C)) if d == 0 else (slice(H, C), slice(0, H))
        edge = bc[H - 1:H, :] if d == 0 else bc[H:H + 1, :]
        atts, dev = [], None
        for rows in (first, second):
            qt, kt, _, dev_h = split_factors(bc[rows], qc[rows], kc[rows])
            att = _dot_nt(qt.astype(BF16), kt.astype(BF16))
            atts.append(jnp.where(mask_h[d], att, 0.0).astype(BF16))
            dev = dev_h if dev is None else jnp.maximum(dev, dev_h)
        cross = _dot_nt((qc[second] * jnp.exp(bc[second] - edge)).astype(BF16),
                        (kc[first] * jnp.exp(edge - bc[first])).astype(BF16)).astype(BF16)
        o_first = _dot(atts[0], vc[first])
        o_second = _dot(atts[1], vc[second]) + _dot(cross, vc[first])
        parts = [o_first, o_second] if d == 0 else [o_second, o_first]
        return jnp.concatenate(parts, axis=0), dev

    def pairwise(d, bc, qc, kc, vf):
        s_idx = lax.broadcasted_iota(jnp.int32, (C, 1), 0)

        def row(t, o_blk):
            sel = s_idx == t
            b_t = jnp.sum(jnp.where(sel, bc, 0.0), axis=0, keepdims=True)
            q_t = jnp.sum(jnp.where(sel, qc, 0.0), axis=0, keepdims=True)
            allowed = (s_idx <= t) if d == 0 else (s_idx >= t)
            decay = jnp.where(allowed, jnp.exp(jnp.where(allowed, b_t - bc, 0.0)), 0.0)
            att_t = jnp.sum(decay * (q_t * kc), axis=1, keepdims=True)
            o_t = jnp.sum(att_t * vf, axis=0, keepdims=True)
            return jnp.where(sel, o_t, o_blk)

        return lax.fori_loop(0, C, row, jnp.zeros((C, HEAD), F32))

    def redo_chunk(chunk, rows):
        bcs = [b_ref[rows, :] for b_ref, _, _, _, _, _ in dirs]
        span = None
        for bc in bcs:
            r = 0.5 * (bc[0:1, :] + bc[C - 1:C, :])
            dev = jnp.maximum(jnp.abs(bc[0:1, :] - r), jnp.abs(r))
            span = dev if span is None else jnp.maximum(span, dev)

        @pl.when(jnp.max(span) > EXP_CLAMP)
        def _():
            qc = q_ref[rows, :].astype(F32)
            vc = v_ref[rows, :]
            kcs = [k_ref[rows, :].astype(F32) for _, k_ref, _, _, _, _ in dirs]
            o_sum, span_h = None, None
            for d, (_, _, last_row, qin_ref, u_ref, _) in enumerate(dirs):
                bc, kc = bcs[d], kcs[d]
                b_last = bc[last_row:last_row + 1, :]
                qin_ref[rows, :] = (qc * jnp.exp(bc)).astype(BF16)
                u_ref[chunk] = _dot_tn(vc, (kc * jnp.exp(b_last - bc)).astype(BF16))
                o_d, dev = halves(d, bc, qc, kc, vc)
                o_sum = o_d if o_sum is None else o_sum + o_d
                span_h = dev if span_h is None else jnp.maximum(span_h, dev)
            o_ref[rows, :] = o_sum

            @pl.when(jnp.max(span_h) > EXP_CLAMP)
            def _():
                vf = vc.astype(F32)
                o_ref[rows, :] = pairwise(0, bcs[0], qc, kcs[0], vf) + pairwise(1, bcs[1], qc, kcs[1], vf)

    def local(j, carry):
        r0 = pl.multiple_of(j * GC, GC)
        q = q_ref[pl.ds(r0, GC), :].astype(F32)
        v = v_ref[pl.ds(r0, GC), :]
        work = []
        span = None
        for d, (b_ref, k_ref, last_row, qin_ref, u_ref, el_ref) in enumerate(dirs):
            b = b_ref[pl.ds(r0, GC), :]
            kk = k_ref[pl.ds(r0, GC), :].astype(F32)
            for c in range(G):
                sl = slice(c * C, (c + 1) * C)
                bc = b[sl]
                b_last = bc[last_row:last_row + 1, :]
                qt, kt, r, dev = split_factors(bc, q[sl], kk[sl])
                dev = jnp.maximum(dev, jnp.abs(r))
                span = dev if span is None else jnp.maximum(span, dev)
                qin_ref[pl.ds(r0 + c * C, C), :] = (qt * jnp.exp(r)).astype(BF16)
                el_ref[j * G + c] = jnp.exp(b_last)
                k_out = (kt * jnp.exp(b_last - r)).astype(BF16)
                work.append((qt.astype(BF16), kt.astype(BF16), k_out, mask_c[d], v[sl], u_ref, c))
        atts = [_dot_nt(qt, kt) for qt, kt, _, _, _, _, _ in work]
        atts = [jnp.where(w[3], a, 0.0).astype(BF16) for w, a in zip(work, atts)]
        for (_, _, k_out, _, vc, u_ref, c) in work:
            u_ref[j * G + c] = _dot_tn(vc, k_out)
        outs = [_dot(a, w[4]) for w, a in zip(work, atts)]
        o_f = jnp.concatenate(outs[:G], axis=0)
        o_b = jnp.concatenate(outs[G:], axis=0)
        o_ref[pl.ds(r0, GC), :] = o_f + o_b

        @pl.when(jnp.max(span) > EXP_CLAMP)
        def _():
            def redo(c, carry):
                redo_chunk(j * G + c, pl.ds(pl.multiple_of(j * GC + c * C, C), C))
                return carry

            lax.fori_loop(0, G, redo, 0)

        return carry

    lax.fori_loop(0, seq_len // GC, local, 0)

    if has_init:
        s_init = (init_ref[0].T, init_ref[1].T)
    else:
        s_init = (jnp.zeros((HEAD, HEAD), F32), jnp.zeros((HEAD, HEAD), F32))

    def step(i, carry):
        s_f, s_b = carry
        jb = n_chunks - 1 - i
        sf_ref[i] = s_f.astype(BF16)
        sb_ref[jb] = s_b.astype(BF16)
        return (s_f * elf_ref[i] + uf_ref[i], s_b * elb_ref[jb] + ub_ref[jb])

    s_f, s_b = lax.fori_loop(0, n_chunks, step, s_init, unroll=SCAN_UNROLL)

    gn = gn_ref[...]

    def add_state_readout(j):
        r0 = pl.multiple_of(j * GC, GC)
        o = o_ref[pl.ds(r0, GC), :]
        for qin_ref, s_ref in ((qinf_ref, sf_ref), (qinb_ref, sb_ref)):
            o = o + jnp.concatenate(
                [_dot_nt(qin_ref[pl.ds(r0 + c * C, C), :], s_ref[j * G + c]) for c in range(G)], axis=0)
        o_ref[pl.ds(r0, GC), :] = o

    def normalise(j):
        r0 = pl.multiple_of(j * GC, GC)
        o = o_ref[pl.ds(r0, GC), :]
        ms = jnp.mean(o * o, axis=-1, keepdims=True)
        o = o * lax.rsqrt(ms + EPS) * gn
        y_ref[pl.ds(r0, GC), :] = (o * z_ref[pl.ds(r0, GC), :].astype(F32)).astype(BF16)

    n_groups = seq_len // GC
    add_state_readout(0)

    def finish(j, carry):
        normalise(j - 1)
        add_state_readout(j)
        return carry

    lax.fori_loop(1, n_groups, finish, 0)
    normalise(n_groups - 1)

    if emit_state:
        st_ref[0] = s_f.T
        st_ref[1] = s_b.T


def _scan(pf, pa, gnorm3, init_state, layer, n_seq, seq_len, emit_state):
    tokens = n_seq * seq_len
    n_chunks = seq_len // SCAN_CHUNK
    has_init = init_state is not None
    blk = lambda c0: pl.BlockSpec((seq_len, HEAD), lambda b, h, c0=c0: (b, c0 + h))
    in_specs = [
        blk(0), blk(N_HEADS),
        blk(PA_Q * N_HEADS), blk(PA_V * N_HEADS), blk(PA_Z * N_HEADS),
        blk(PA_KF * N_HEADS), blk((PA_KF + 1) * N_HEADS),
        pl.BlockSpec((None, 1, HEAD), lambda b, h: (layer, 0, h)),
    ]
    args = [pf, pf, pa, pa, pa, pa, pa, gnorm3]
    if has_init:
        in_specs.append(pl.BlockSpec((None, None, 2, None, HEAD, HEAD),
                                     lambda b, h: (b, layer, 0, h, 0, 0)))
        args.append(init_state)
    out_shape = [jax.ShapeDtypeStruct((tokens, D_A), BF16)]
    out_specs = [pl.BlockSpec((seq_len, HEAD), lambda b, h: (b, h))]
    if emit_state:
        out_shape.append(jax.ShapeDtypeStruct((n_seq, 2, N_HEADS, HEAD, HEAD), F32))
        out_specs.append(pl.BlockSpec((None, 2, None, HEAD, HEAD), lambda b, h: (b, 0, h, 0, 0)))
    res = pl.pallas_call(
        functools.partial(_scan_kernel, seq_len, has_init, emit_state),
        grid=(n_seq, N_HEADS),
        in_specs=in_specs,
        out_specs=out_specs,
        out_shape=out_shape,
        scratch_shapes=[
            pltpu.VMEM((seq_len, HEAD), F32),
            pltpu.VMEM((n_chunks, HEAD, HEAD), BF16), pltpu.VMEM((n_chunks, HEAD, HEAD), BF16),
            pltpu.VMEM((seq_len, HEAD), BF16), pltpu.VMEM((seq_len, HEAD), BF16),
            pltpu.VMEM((n_chunks, HEAD, HEAD), F32), pltpu.VMEM((n_chunks, HEAD, HEAD), F32),
            pltpu.VMEM((n_chunks, 1, HEAD), F32), pltpu.VMEM((n_chunks, 1, HEAD), F32),
        ],
        compiler_params=_cparams(2),
        name="scan",
    )(*args)
    return res if emit_state else (res[0], None)


def _fourier_dense_kernel(scale, zr_ref, zi_ref, c_ref, s_ref, y_ref):
    yf = (_dot(c_ref[...].astype(BF16), zr_ref[...].astype(BF16))
          + _dot(s_ref[...].astype(BF16), zi_ref[...].astype(BF16)))
    y_ref[...] = (yf * scale).astype(BF16)


def _fourier_dense(zr, zi, n_seq, seq_len):
    c, s = _dft_cos_sin(seq_len)
    scale = 1.0 / np.sqrt(seq_len * GROUP)
    blk = pl.BlockSpec((seq_len, D_B), lambda b: (b, 0))
    mat = pl.BlockSpec((seq_len, seq_len), lambda b: (0, 0))
    return pl.pallas_call(
        functools.partial(_fourier_dense_kernel, scale),
        grid=(n_seq,),
        in_specs=[blk, blk, mat, mat],
        out_specs=blk,
        out_shape=jax.ShapeDtypeStruct((n_seq * seq_len, D_B), BF16),
        compiler_params=_cparams(1),
        name="fourier_dense",
    )(zr, zi, jnp.asarray(c, F32), jnp.asarray(s, F32))


def _fourier_stage1_kernel(zr_ref, zi_ref, m_ref, tc_ref, ts_ref, br_ref, bi_ref):
    m = m_ref[...].astype(BF16)
    for n in range(FFT_BLOCK):
        z = jnp.concatenate([zr_ref[:, n, :], zi_ref[:, n, :]], axis=0).astype(BF16)
        a = _dot(m, z)
        ar = a[:FFT_RADIX]
        ai = a[FFT_RADIX:]
        tc = jnp.concatenate([tc_ref[n]] * N_GROUPS, axis=1)
        ts = jnp.concatenate([ts_ref[n]] * N_GROUPS, axis=1)
        br_ref[:, n, :] = ar * tc + ai * ts
        bi_ref[:, n, :] = ai * tc - ar * ts


def _fourier_stage2_kernel(scale, br_ref, bi_ref, cs_ref, y_ref):
    cs = cs_ref[...].astype(BF16)
    for j in range(FFT_BLOCK):
        bcat = jnp.concatenate([br_ref[j], bi_ref[j]], axis=0).astype(BF16)
        y_ref[:, j, :] = _dot(cs, bcat) * scale


def _fourier_two_stage(zr, zi, n_seq, seq_len):
    R = FFT_RADIX
    assert seq_len == R * R
    c, s = _dft_cos_sin(R)
    m1 = np.block([[c, s], [-s, c]])
    idx = np.arange(R, dtype=np.float64)
    ang = 2.0 * np.pi * np.outer(idx, idx) / seq_len
    tc = np.repeat(np.cos(ang)[:, :, None], GROUP, axis=2).astype(np.float32)
    ts = np.repeat(np.sin(ang)[:, :, None], GROUP, axis=2).astype(np.float32)
    cs = np.concatenate([c, s], axis=1)
    scale = 1.0 / np.sqrt(seq_len * GROUP)

    fast_blk = pl.BlockSpec((None, R, FFT_BLOCK, D_B), lambda b, j: (b, 0, j, 0))
    slow_blk = pl.BlockSpec((None, FFT_BLOCK, R, D_B), lambda b, j: (b, j, 0, 0))
    tw_blk = pl.BlockSpec((FFT_BLOCK, R, GROUP), lambda b, j: (j, 0, 0))
    shape4 = jax.ShapeDtypeStruct((n_seq, R, R, D_B), F32)
    br, bi = pl.pallas_call(
        _fourier_stage1_kernel,
        grid=(n_seq, R // FFT_BLOCK),
        in_specs=[fast_blk, fast_blk, pl.BlockSpec((2 * R, 2 * R), lambda b, j: (0, 0)), tw_blk, tw_blk],
        out_specs=(fast_blk, fast_blk),
        out_shape=(shape4, shape4),
        compiler_params=_cparams(2),
        name="fourier_stage1",
    )(zr.reshape(n_seq, R, R, D_B), zi.reshape(n_seq, R, R, D_B),
      jnp.asarray(m1, F32), jnp.asarray(tc), jnp.asarray(ts))

    y = pl.pallas_call(
        functools.partial(_fourier_stage2_kernel, scale),
        grid=(n_seq, R // FFT_BLOCK),
        in_specs=[slow_blk, slow_blk, pl.BlockSpec((R, 2 * R), lambda b, j: (0, 0))],
        out_specs=fast_blk,
        out_shape=shape4,
        compiler_params=_cparams(2),
        name="fourier_stage2",
    )(br, bi, jnp.asarray(cs, F32))
    return y.reshape(n_seq * seq_len, D_B)


def _merge_kernel(has_pe, final, *refs):
    refs = list(refs)
    ya_ref, yf_ref, zb_ref, ga_ref, gb_ref, x_ref = refs[:6]
    pos = 6
    x = x_ref[...]
    if has_pe:
        x = x + refs[pos][...]
        pos += 1
    mod_ref, wpa_ref, wpb_ref, wo_ref = refs[pos:pos + 4]
    pos += 4
    if final:
        fw_ref = refs[pos]
        pos += 1
    out_ref = refs[pos]
    yb = (yf_ref[...].astype(F32) * zb_ref[...].astype(F32)).astype(BF16)
    merged = (ga_ref[...].astype(F32) * _dot(ya_ref[...], wpa_ref[...])
              + gb_ref[...].astype(F32) * _dot(yb, wpb_ref[...]))
    out = _dot(merged.astype(BF16), wo_ref[...])
    xn = x + mod_ref[:, 2 * D:3 * D] * out
    if final:
        ms = jnp.mean(xn * xn, axis=-1, keepdims=True)
        xn = xn * lax.rsqrt(ms + EPS) * fw_ref[...]
    out_ref[...] = xn


def _merge(ya, yf, zb, pa, x, pe, mod, wpa, wpb, wo, final_w, layer, seq_len, latent):
    tokens = x.shape[0]
    tm = TM_MERGE
    has_pe = pe is not None
    final = final_w is not None
    row = lambda i: (i, 0)
    if latent:
        mod_idx = lambda i: (layer * MOD_ROWS + 1 + (i * tm) // seq_len, 0, 0)
    else:
        mod_idx = lambda i: (layer * MOD_ROWS, 0, 0)
    in_specs = [
        pl.BlockSpec((tm, D_A), row),
        pl.BlockSpec((tm, D_B), row),
        pl.BlockSpec((tm, D_B), row),
        pl.BlockSpec((tm, D), lambda i: (i, PA_GA)),
        pl.BlockSpec((tm, D), lambda i: (i, PA_GB)),
        pl.BlockSpec((tm, D), row),
    ]
    args = [ya, yf, zb, pa, pa, x]
    if has_pe:
        per_seq = seq_len // tm
        in_specs.append(pl.BlockSpec((tm, D), lambda i: (i % per_seq, 0)))
        args.append(pe)
    in_specs += [
        pl.BlockSpec((None, 1, 3 * D), mod_idx),
        pl.BlockSpec((None, D_A, D), lambda i: (layer, 0, 0)),
        pl.BlockSpec((None, D_B, D), lambda i: (layer, 0, 0)),
        pl.BlockSpec((None, D, D), lambda i: (layer, 0, 0)),
    ]
    args += [mod, wpa, wpb, wo]
    if final:
        in_specs.append(pl.BlockSpec((1, D), lambda i: (0, 0)))
        args.append(final_w)
    return pl.pallas_call(
        functools.partial(_merge_kernel, has_pe, final),
        grid=(tokens // tm,),
        in_specs=in_specs,
        out_specs=pl.BlockSpec((tm, D), row),
        out_shape=jax.ShapeDtypeStruct((tokens, D), F32),
        compiler_params=_cparams(1),
        name="merge",
    )(*args)


def kernel(x_prompt, x_sample, state_hgrn, c, c_ctx, norm_w, w_ada, b_ada, w_in, lb_raw,
           gnorm_w, w_pa, w_pb, w_o, final_norm_w):
    n_p, len_p, _ = x_prompt.shape
    n_s, len_s, _ = x_sample.shape

    cond = jnp.zeros((MOD_ROWS, D), F32).at[0].set(c_ctx).at[1:1 + n_s].set(c)
    mod = _modulation(cond, w_ada, b_ada)
    lbs = _lower_bounds(lb_raw)

    w_in_bf = w_in.astype(BF16)
    wpa_bf = w_pa.astype(BF16)
    wpb_bf = w_pb.astype(BF16)
    wo_bf = w_o.astype(BF16)
    norm_w3 = norm_w.reshape(DEPTH, 1, D)
    gnorm3 = gnorm_w.reshape(DEPTH, 1, D_A)
    final_w = final_norm_w.reshape(1, D)

    cc_np, sc_np = _dft_cos_sin(GROUP)
    cc = jnp.asarray(cc_np, F32)
    sc = jnp.asarray(sc_np, F32)
    pe = jnp.asarray(_pos_embed_table(len_s, D))

    xp = x_prompt.reshape(n_p * len_p, D)
    xs = x_sample.reshape(n_s * len_s, D)
    states = []
    for l in range(DEPTH):
        last = l == DEPTH - 1
        fw = final_w if last else None
        pe_l = pe if l == 0 else None

        pf, pa, zr, zi, zb = _proj(xp, None, mod, norm_w3, lbs, w_in_bf, cc, sc, l, len_p, False)
        ya, st = _scan(pf, pa, gnorm3, None, l, n_p, len_p, True)
        yf = _fourier_dense(zr, zi, n_p, len_p)
        xp = _merge(ya, yf, zb, pa, xp, None, mod, wpa_bf, wpb_bf, wo_bf, fw, l, len_p, False)
        states.append(st)

        pf, pa, zr, zi, zb = _proj(xs, pe_l, mod, norm_w3, lbs, w_in_bf, cc, sc, l, len_s, True)
        ya, _ = _scan(pf, pa, gnorm3, state_hgrn, l, n_s, len_s, False)
        yf = _fourier_two_stage(zr, zi, n_s, len_s)
        xs = _merge(ya, yf, zb, pa, xs, pe_l, mod, wpa_bf, wpb_bf, wo_bf, fw, l, len_s, True)

    y_prompt = xp.reshape(n_p, len_p, D)
    y_sample = xs.reshape(n_s, len_s, D)
    new_state = jnp.stack(states, axis=1)
    return (y_prompt, y_sample, new_state)
```

```python
import functools

import jax
import jax.numpy as jnp
import numpy as np
from jax import lax
from jax.experimental import pallas as pl
from jax.experimental.pallas import tpu as pltpu

D = 1024
DEPTH = 4
N_HEADS = 8
HEAD = 128
D_A = N_HEADS * HEAD
N_GROUPS = 4
GROUP = 128
D_B = N_GROUPS * GROUP
D_IN = 5 * D_A + 2 * D_B + 2 * D
GRID_W = 64
EPS = 1e-6

V7X_VMEM_LIMIT_BYTES = 56 * 1024 * 1024

SCAN_CHUNK = 64
SCAN_GROUP = 8
SCAN_UNROLL = 4
SCAN_SHORT_SEQ = 512
SCAN_HEADS_SHORT = 2
EXP_CLAMP = 80.0
TM_PROJ = 256
TM_MERGE = 512
FFT_RADIX = 64
FFT_BLOCK = 8
MOD_ROWS = 8

F32 = jnp.float32
BF16 = jnp.bfloat16


def _cparams(n_axes):
    return pltpu.CompilerParams(
        dimension_semantics=("arbitrary",) * n_axes,
        vmem_limit_bytes=V7X_VMEM_LIMIT_BYTES,
    )


def _dot(a, b):
    return jnp.dot(a, b, preferred_element_type=F32)


def _dot_nt(a, b):
    return lax.dot_general(a, b, (((1,), (1,)), ((), ())), preferred_element_type=F32)


def _dot_tn(a, b):
    return lax.dot_general(a, b, (((0,), (0,)), ((), ())), preferred_element_type=F32)


def _sigmoid(x):
    return 1.0 / (1.0 + jnp.exp(-x))


def _silu(x):
    return x * _sigmoid(x)


def _dft_cos_sin(n):
    k = np.arange(n, dtype=np.float64)
    ang = 2.0 * np.pi * np.outer(k, k) / n
    return np.cos(ang), np.sin(ang)


def _pos_embed_table(length, d):
    rows = length // GRID_W
    r = np.repeat(np.arange(rows, dtype=np.float64), GRID_W)
    col = np.tile(np.arange(GRID_W, dtype=np.float64), rows)
    nf = d // 4
    freqs = 1.0 / (10000.0 ** (np.arange(nf, dtype=np.float64) / nf))

    def emb(p):
        a = p[:, None] * freqs[None, :]
        return np.concatenate([np.sin(a), np.cos(a)], axis=-1)

    return np.concatenate([emb(r), emb(col)], axis=-1).astype(np.float32)


def _lb_kernel(raw_ref, out_ref):
    x = raw_ref[...]
    m = jnp.max(x, axis=0, keepdims=True)
    e = jnp.exp(x - m)
    p = e / jnp.sum(e, axis=0, keepdims=True)
    cs = p[0:1]
    first = cs
    out_ref[0:1, :] = cs - first
    for l in range(1, DEPTH):
        cs = cs + p[l:l + 1]
        out_ref[l:l + 1, :] = cs - first


def _lower_bounds(lb_raw):
    raw = lb_raw.reshape(DEPTH, 2 * D_A)
    out = pl.pallas_call(
        _lb_kernel,
        out_shape=jax.ShapeDtypeStruct((DEPTH, 2 * D_A), F32),
        name="lower_bounds",
    )(raw)
    return out.reshape(DEPTH * 2, 1, D_A)


def _mod_kernel(cond_ref, w_ref, b_ref, out_ref):
    a = _silu(cond_ref[...])
    out_ref[...] = jnp.dot(a, w_ref[...], preferred_element_type=F32,
                           precision=lax.Precision.HIGHEST) + b_ref[...]


def _modulation(cond, w_ada, b_ada):
    tn = 512
    out = pl.pallas_call(
        _mod_kernel,
        grid=(DEPTH, 3 * D // tn),
        in_specs=[
            pl.BlockSpec((MOD_ROWS, D), lambda l, j: (0, 0)),
            pl.BlockSpec((None, D, tn), lambda l, j: (l, 0, j)),
            pl.BlockSpec((None, 1, tn), lambda l, j: (l, 0, j)),
        ],
        out_specs=pl.BlockSpec((None, MOD_ROWS, tn), lambda l, j: (l, 0, j)),
        out_shape=jax.ShapeDtypeStruct((DEPTH, MOD_ROWS, 3 * D), F32),
        compiler_params=_cparams(2),
        name="modulation",
    )(cond, w_ada, b_ada.reshape(DEPTH, 1, 3 * D))
    return out.reshape(DEPTH * MOD_ROWS, 1, 3 * D)


_Q0, _FF0, _V0, _ZA0, _U0, _ZB0, _GA0, _GB0 = (
    0, D_A, 3 * D_A, 4 * D_A, 5 * D_A, 5 * D_A + D_B, 5 * D_A + 2 * D_B, 5 * D_A + 2 * D_B + D)
PA_Q, PA_V, PA_Z, PA_GA, PA_GB, PA_KF = 0, 1, 2, 3, 4, 5
PA_WIDTH = 7 * D


def _forget_gate(x, lb):
    f = lb + (1.0 - lb) * _sigmoid(x)
    log_f = jnp.where(f < 1e-30, jnp.minimum(x, 0.0), jnp.log(jnp.maximum(f, 1e-30)))
    return log_f, 1.0 - f


def _proj_kernel(has_pe, *refs):
    refs = list(refs)
    x = refs[0][...]
    pos = 1
    if has_pe:
        x = x + refs[pos][...]
        pos += 1
    mod_ref, nw_ref, lbf_ref, lbb_ref, w_ref, cc_ref, sc_ref = refs[pos:pos + 7]
    pf_ref, pa_ref, zr_ref, zi_ref, zb_ref = refs[pos + 7:]

    ms = jnp.mean(x * x, axis=-1, keepdims=True)
    y = x * lax.rsqrt(ms + EPS) * nw_ref[...]
    shift = mod_ref[:, 0:D]
    scale = mod_ref[:, D:2 * D]
    h = (y * (1.0 + scale) + shift).astype(BF16)

    def sec(c0, width):
        return _dot(h, w_ref[:, c0:c0 + width])

    def put(block, val):
        pa_ref[:, block * D:(block + 1) * D] = val.astype(BF16)

    tm = x.shape[0]
    row = lax.broadcasted_iota(jnp.int32, (tm, tm), 0)
    col = lax.broadcasted_iota(jnp.int32, (tm, tm), 1)
    same_chunk = (row // SCAN_CHUNK) == (col // SCAN_CHUNK)
    tris = (jnp.where(same_chunk & (col <= row), 1.0, 0.0).astype(BF16),
            jnp.where(same_chunk & (col >= row), 1.0, 0.0).astype(BF16))
    for d, lb_ref in enumerate((lbf_ref, lbb_ref)):
        g, kk = _forget_gate(sec(_FF0 + d * D_A, D_A), lb_ref[...])
        g_hi = g.astype(BF16)
        g_lo = (g - g_hi.astype(F32)).astype(BF16)
        pf_ref[:, d * D_A:(d + 1) * D_A] = _dot(tris[d], g_hi) + _dot(tris[d], g_lo)
        put(PA_KF + d, kk)

    put(PA_Q, _silu(sec(_Q0, D_A)))
    put(PA_V, sec(_V0, D_A))
    put(PA_Z, _silu(sec(_ZA0, D_A)))
    put(PA_GA, _sigmoid(sec(_GA0, D)))
    put(PA_GB, _sigmoid(sec(_GB0, D)))
    zb_ref[...] = _silu(sec(_ZB0, D_B)).astype(BF16)
    u = sec(_U0, D_B).astype(BF16)
    cc = cc_ref[...].astype(BF16)
    sc = sc_ref[...].astype(BF16)
    for g in range(N_GROUPS):
        ug = u[:, g * GROUP:(g + 1) * GROUP]
        zr_ref[:, g * GROUP:(g + 1) * GROUP] = _dot(ug, cc)
        zi_ref[:, g * GROUP:(g + 1) * GROUP] = -_dot(ug, sc)


def _proj(x, pe, mod, norm_w3, lbs, w_in_bf, cc, sc, layer, seq_len, latent):
    tokens = x.shape[0]
    tm = TM_PROJ
    has_pe = pe is not None
    per_seq = seq_len // tm
    if latent:
        mod_idx = lambda i: (layer * MOD_ROWS + 1 + i // per_seq, 0, 0)
    else:
        mod_idx = lambda i: (layer * MOD_ROWS, 0, 0)
    in_specs = [pl.BlockSpec((tm, D), lambda i: (i, 0))]
    args = [x]
    if has_pe:
        in_specs.append(pl.BlockSpec((tm, D), lambda i: (i % per_seq, 0)))
        args.append(pe)
    in_specs += [
        pl.BlockSpec((None, 1, 3 * D), mod_idx),
        pl.BlockSpec((None, 1, D), lambda i: (layer, 0, 0)),
        pl.BlockSpec((None, 1, D_A), lambda i: (2 * layer, 0, 0)),
        pl.BlockSpec((None, 1, D_A), lambda i: (2 * layer + 1, 0, 0)),
        pl.BlockSpec((None, D, D_IN), lambda i: (layer, 0, 0), pipeline_mode=pl.Buffered(1)),
        pl.BlockSpec((GROUP, GROUP), lambda i: (0, 0)),
        pl.BlockSpec((GROUP, GROUP), lambda i: (0, 0)),
    ]
    args += [mod, norm_w3, lbs, lbs, w_in_bf, cc, sc]
    out_shape = (
        jax.ShapeDtypeStruct((tokens, 2 * D_A), F32),
        jax.ShapeDtypeStruct((tokens, PA_WIDTH), BF16),
        jax.ShapeDtypeStruct((tokens, D_B), F32),
        jax.ShapeDtypeStruct((tokens, D_B), F32),
        jax.ShapeDtypeStruct((tokens, D_B), BF16),
    )
    out_specs = (
        pl.BlockSpec((tm, 2 * D_A), lambda i: (i, 0)),
        pl.BlockSpec((tm, PA_WIDTH), lambda i: (i, 0)),
        pl.BlockSpec((tm, D_B), lambda i: (i, 0)),
        pl.BlockSpec((tm, D_B), lambda i: (i, 0)),
        pl.BlockSpec((tm, D_B), lambda i: (i, 0)),
    )
    return pl.pallas_call(
        functools.partial(_proj_kernel, has_pe),
        grid=(tokens // tm,),
        in_specs=in_specs,
        out_specs=out_specs,
        out_shape=out_shape,
        compiler_params=_cparams(1),
        name="proj",
    )(*args)


def _scan_kernel(seq_len, heads, has_init, emit_state, *refs):
    refs = list(refs)
    bf_ref, bb_ref, q_ref, v_ref, z_ref, kf_ref, kb_ref, gn_ref = refs[:8]
    pos = 8
    init_ref = None
    if has_init:
        init_ref = refs[pos]
        pos += 1
    y_ref = refs[pos]
    pos += 1
    st_ref = None
    if emit_state:
        st_ref = refs[pos]
        pos += 1
    (o_ref, sf_ref, sb_ref, qinf_ref, qinb_ref, uf_ref, ub_ref, elf_ref, elb_ref) = refs[pos:pos + 9]

    C = SCAN_CHUNK
    H = C // 2
    n_chunks = seq_len // C
    G = min(SCAN_GROUP, n_chunks)
    GC = G * C
    n_groups = seq_len // GC
    lanes = [slice(hd * HEAD, (hd + 1) * HEAD) for hd in range(heads)]

    def pair_masks(n):
        rr = lax.broadcasted_iota(jnp.int32, (n, n), 0)
        cc = lax.broadcasted_iota(jnp.int32, (n, n), 1)
        return cc <= rr, cc >= rr

    mask_c = pair_masks(C)
    mask_h = pair_masks(H)
    dirs = (
        (bf_ref, kf_ref, C - 1, qinf_ref, uf_ref, elf_ref),
        (bb_ref, kb_ref, 0, qinb_ref, ub_ref, elb_ref),
    )

    def split_factors(bq, qv, kv):
        n = bq.shape[0]
        r = 0.5 * (bq[0:1, :] + bq[n - 1:n, :])
        e = jnp.exp(jnp.clip(bq - r, -EXP_CLAMP, EXP_CLAMP))
        return qv * e, kv * (1.0 / e), r, jnp.abs(bq[0:1, :] - r)

    def halves(d, bc, qc, kc, vc):
        first, second = (slice(0, H), slice(H, C)) if d == 0 else (slice(H, C), slice(0, H))
        edge = bc[H - 1:H, :] if d == 0 else bc[H:H + 1, :]
        atts, dev = [], None
        for rows in (first, second):
            qt, kt, _, dev_h = split_factors(bc[rows], qc[rows], kc[rows])
            att = _dot_nt(qt.astype(BF16), kt.astype(BF16))
            atts.append(jnp.where(mask_h[d], att, 0.0).astype(BF16))
            dev = dev_h if dev is None else jnp.maximum(dev, dev_h)
        cross = _dot_nt((qc[second] * jnp.exp(bc[second] - edge)).astype(BF16),
                        (kc[first] * jnp.exp(edge - bc[first])).astype(BF16)).astype(BF16)
        o_first = _dot(atts[0], vc[first])
        o_second = _dot(atts[1], vc[second]) + _dot(cross, vc[first])
        parts = [o_first, o_second] if d == 0 else [o_second, o_first]
        return jnp.concatenate(parts, axis=0), dev

    def pairwise(d, bc, qc, kc, vf):
        s_idx = lax.broadcasted_iota(jnp.int32, (C, 1), 0)

        def row(t, o_blk):
            sel = s_idx == t
            b_t = jnp.sum(jnp.where(sel, bc, 0.0), axis=0, keepdims=True)
            q_t = jnp.sum(jnp.where(sel, qc, 0.0), axis=0, keepdims=True)
            allowed = (s_idx <= t) if d == 0 else (s_idx >= t)
            decay = jnp.where(allowed, jnp.exp(jnp.where(allowed, b_t - bc, 0.0)), 0.0)
            att_t = jnp.sum(decay * (q_t * kc), axis=1, keepdims=True)
            o_t = jnp.sum(att_t * vf, axis=0, keepdims=True)
            return jnp.where(sel, o_t, o_blk)

        return lax.fori_loop(0, C, row, jnp.zeros((C, HEAD), F32))

    def redo_chunk(hd, chunk, rows):
        ls = lanes[hd]
        bcs = [b_ref[rows, ls] for b_ref, _, _, _, _, _ in dirs]
        span = None
        for bc in bcs:
            r = 0.5 * (bc[0:1, :] + bc[C - 1:C, :])
            dev = jnp.maximum(jnp.abs(bc[0:1, :] - r), jnp.abs(r))
            span = dev if span is None else jnp.maximum(span, dev)

        @pl.when(jnp.max(span) > EXP_CLAMP)
        def _():
            qc = q_ref[rows, ls].astype(F32)
            vc = v_ref[rows, ls]
            kcs = [k_ref[rows, ls].astype(F32) for _, k_ref, _, _, _, _ in dirs]
            o_sum, span_h = None, None
            for d, (_, _, last_row, qin_ref, u_ref, _) in enumerate(dirs):
                bc, kc = bcs[d], kcs[d]
                b_last = bc[last_row:last_row + 1, :]
                qin_ref[rows, ls] = (qc * jnp.exp(bc)).astype(BF16)
                u_ref[hd * n_chunks + chunk] = _dot_tn(vc, (kc * jnp.exp(b_last - bc)).astype(BF16))
                o_d, dev = halves(d, bc, qc, kc, vc)
                o_sum = o_d if o_sum is None else o_sum + o_d
                span_h = dev if span_h is None else jnp.maximum(span_h, dev)
            o_ref[rows, ls] = o_sum

            @pl.when(jnp.max(span_h) > EXP_CLAMP)
            def _():
                vf = vc.astype(F32)
                o_ref[rows, ls] = pairwise(0, bcs[0], qc, kcs[0], vf) + pairwise(1, bcs[1], qc, kcs[1], vf)

    def local(j, carry):
        r0 = pl.multiple_of(j * GC, GC)
        work = []
        span = None
        for hd, ls in enumerate(lanes):
            q = q_ref[pl.ds(r0, GC), ls].astype(F32)
            v = v_ref[pl.ds(r0, GC), ls]
            for d, (b_ref, k_ref, last_row, qin_ref, u_ref, el_ref) in enumerate(dirs):
                b = b_ref[pl.ds(r0, GC), ls]
                kk = k_ref[pl.ds(r0, GC), ls].astype(F32)
                for c in range(G):
                    sl = slice(c * C, (c + 1) * C)
                    bc = b[sl]
                    b_last = bc[last_row:last_row + 1, :]
                    qt, kt, r, dev = split_factors(bc, q[sl], kk[sl])
                    dev = jnp.maximum(dev, jnp.abs(r))
                    span = dev if span is None else jnp.maximum(span, dev)
                    qin_ref[pl.ds(r0 + c * C, C), ls] = (qt * jnp.exp(r)).astype(BF16)
                    el_ref[hd * n_chunks + j * G + c] = jnp.exp(b_last)
                    k_out = (kt * jnp.exp(b_last - r)).astype(BF16)
                    work.append((qt.astype(BF16), kt.astype(BF16), k_out, mask_c[d], v[sl], u_ref,
                                 hd * n_chunks + j * G + c))
        atts = [_dot_nt(qt, kt) for qt, kt, _, _, _, _, _ in work]
        atts = [jnp.where(w[3], a, 0.0).astype(BF16) for w, a in zip(work, atts)]
        for (_, _, k_out, _, vc, u_ref, slot) in work:
            u_ref[slot] = _dot_tn(vc, k_out)
        outs = [_dot(a, w[4]) for w, a in zip(work, atts)]
        for hd, ls in enumerate(lanes):
            o_f = jnp.concatenate(outs[(2 * hd) * G:(2 * hd + 1) * G], axis=0)
            o_b = jnp.concatenate(outs[(2 * hd + 1) * G:(2 * hd + 2) * G], axis=0)
            o_ref[pl.ds(r0, GC), ls] = o_f + o_b

        @pl.when(jnp.max(span) > EXP_CLAMP)
        def _():
            for hd in range(heads):
                def redo(c, carry, hd=hd):
                    redo_chunk(hd, j * G + c, pl.ds(pl.multiple_of(j * GC + c * C, C), C))
                    return carry

                lax.fori_loop(0, G, redo, 0)

        return carry

    lax.fori_loop(0, n_groups, local, 0)

    for hd in range(heads):
        base = hd * n_chunks
        if has_init:
            s_init = (init_ref[0, hd].T, init_ref[1, hd].T)
        else:
            s_init = (jnp.zeros((HEAD, HEAD), F32), jnp.zeros((HEAD, HEAD), F32))

        def step(i, carry, base=base):
            s_f, s_b = carry
            kf = base + i
            kb = base + n_chunks - 1 - i
            sf_ref[kf] = s_f.astype(BF16)
            sb_ref[kb] = s_b.astype(BF16)
            return (s_f * elf_ref[kf] + uf_ref[kf], s_b * elb_ref[kb] + ub_ref[kb])

        s_f, s_b = lax.fori_loop(0, n_chunks, step, s_init, unroll=SCAN_UNROLL)
        if emit_state:
            st_ref[0, hd] = s_f.T
            st_ref[1, hd] = s_b.T

    def add_state_readout(j):
        r0 = pl.multiple_of(j * GC, GC)
        for hd, ls in enumerate(lanes):
            o = o_ref[pl.ds(r0, GC), ls]
            for qin_ref, s_ref in ((qinf_ref, sf_ref), (qinb_ref, sb_ref)):
                o = o + jnp.concatenate(
                    [_dot_nt(qin_ref[pl.ds(r0 + c * C, C), ls], s_ref[hd * n_chunks + j * G + c])
                     for c in range(G)], axis=0)
            o_ref[pl.ds(r0, GC), ls] = o

    def normalise(j):
        r0 = pl.multiple_of(j * GC, GC)
        for ls in lanes:
            o = o_ref[pl.ds(r0, GC), ls]
            ms = jnp.mean(o * o, axis=-1, keepdims=True)
            o = o * lax.rsqrt(ms + EPS) * gn_ref[:, ls]
            y_ref[pl.ds(r0, GC), ls] = (o * z_ref[pl.ds(r0, GC), ls].astype(F32)).astype(BF16)

    add_state_readout(0)

    def finish(j, carry):
        normalise(j - 1)
        add_state_readout(j)
        return carry

    lax.fori_loop(1, n_groups, finish, 0)
    normalise(n_groups - 1)


def _scan(pf, pa, gnorm3, init_state, layer, n_seq, seq_len, emit_state):
    tokens = n_seq * seq_len
    n_chunks = seq_len // SCAN_CHUNK
    has_init = init_state is not None
    heads = SCAN_HEADS_SHORT if seq_len <= SCAN_SHORT_SEQ else 1
    width = heads * HEAD
    blk = lambda c0: pl.BlockSpec((seq_len, width), lambda b, h, c0=c0: (b, c0 // heads + h))
    in_specs = [
        blk(0), blk(N_HEADS),
        blk(PA_Q * N_HEADS), blk(PA_V * N_HEADS), blk(PA_Z * N_HEADS),
        blk(PA_KF * N_HEADS), blk((PA_KF + 1) * N_HEADS),
        pl.BlockSpec((None, 1, width), lambda b, h: (layer, 0, h)),
    ]
    args = [pf, pf, pa, pa, pa, pa, pa, gnorm3]
    if has_init:
        in_specs.append(pl.BlockSpec((None, None, 2, heads, HEAD, HEAD),
                                     lambda b, h: (b, layer, 0, h, 0, 0)))
        args.append(init_state)
    out_shape = [jax.ShapeDtypeStruct((tokens, D_A), BF16)]
    out_specs = [pl.BlockSpec((seq_len, width), lambda b, h: (b, h))]
    if emit_state:
        out_shape.append(jax.ShapeDtypeStruct((n_seq, 2, N_HEADS, HEAD, HEAD), F32))
        out_specs.append(pl.BlockSpec((None, 2, heads, HEAD, HEAD), lambda b, h: (b, 0, h, 0, 0)))
    slots = heads * n_chunks
    res = pl.pallas_call(
        functools.partial(_scan_kernel, seq_len, heads, has_init, emit_state),
        grid=(n_seq, N_HEADS // heads),
        in_specs=in_specs,
        out_specs=out_specs,
        out_shape=out_shape,
        scratch_shapes=[
            pltpu.VMEM((seq_len, width), F32),
            pltpu.VMEM((slots, HEAD, HEAD), BF16), pltpu.VMEM((slots, HEAD, HEAD), BF16),
            pltpu.VMEM((seq_len, width), BF16), pltpu.VMEM((seq_len, width), BF16),
            pltpu.VMEM((slots, HEAD, HEAD), F32), pltpu.VMEM((slots, HEAD, HEAD), F32),
            pltpu.VMEM((slots, 1, HEAD), F32), pltpu.VMEM((slots, 1, HEAD), F32),
        ],
        compiler_params=_cparams(2),
        name="scan",
    )(*args)
    return res if emit_state else (res[0], None)


def _fourier_dense_kernel(scale, zr_ref, zi_ref, c_ref, s_ref, y_ref):
    yf = (_dot(c_ref[...].astype(BF16), zr_ref[...].astype(BF16))
          + _dot(s_ref[...].astype(BF16), zi_ref[...].astype(BF16)))
    y_ref[...] = (yf * scale).astype(BF16)


def _fourier_dense(zr, zi, n_seq, seq_len):
    c, s = _dft_cos_sin(seq_len)
    scale = 1.0 / np.sqrt(seq_len * GROUP)
    blk = pl.BlockSpec((seq_len, D_B), lambda b: (b, 0))
    mat = pl.BlockSpec((seq_len, seq_len), lambda b: (0, 0))
    return pl.pallas_call(
        functools.partial(_fourier_dense_kernel, scale),
        grid=(n_seq,),
        in_specs=[blk, blk, mat, mat],
        out_specs=blk,
        out_shape=jax.ShapeDtypeStruct((n_seq * seq_len, D_B), BF16),
        compiler_params=_cparams(1),
        name="fourier_dense",
    )(zr, zi, jnp.asarray(c, F32), jnp.asarray(s, F32))


def _fourier_stage1_kernel(zr_ref, zi_ref, m_ref, tc_ref, ts_ref, br_ref, bi_ref):
    m = m_ref[...].astype(BF16)
    for n in range(FFT_BLOCK):
        z = jnp.concatenate([zr_ref[:, n, :], zi_ref[:, n, :]], axis=0).astype(BF16)
        a = _dot(m, z)
        ar = a[:FFT_RADIX]
        ai = a[FFT_RADIX:]
        tc = jnp.concatenate([tc_ref[n]] * N_GROUPS, axis=1)
        ts = jnp.concatenate([ts_ref[n]] * N_GROUPS, axis=1)
        br_ref[:, n, :] = ar * tc + ai * ts
        bi_ref[:, n, :] = ai * tc - ar * ts


def _fourier_stage2_kernel(scale, br_ref, bi_ref, cs_ref, y_ref):
    cs = cs_ref[...].astype(BF16)
    for j in range(FFT_BLOCK):
        bcat = jnp.concatenate([br_ref[j], bi_ref[j]], axis=0).astype(BF16)
        y_ref[:, j, :] = _dot(cs, bcat) * scale


def _fourier_two_stage(zr, zi, n_seq, seq_len):
    R = FFT_RADIX
    assert seq_len == R * R
    c, s = _dft_cos_sin(R)
    m1 = np.block([[c, s], [-s, c]])
    idx = np.arange(R, dtype=np.float64)
    ang = 2.0 * np.pi * np.outer(idx, idx) / seq_len
    tc = np.repeat(np.cos(ang)[:, :, None], GROUP, axis=2).astype(np.float32)
    ts = np.repeat(np.sin(ang)[:, :, None], GROUP, axis=2).astype(np.float32)
    cs = np.concatenate([c, s], axis=1)
    scale = 1.0 / np.sqrt(seq_len * GROUP)

    fast_blk = pl.BlockSpec((None, R, FFT_BLOCK, D_B), lambda b, j: (b, 0, j, 0))
    slow_blk = pl.BlockSpec((None, FFT_BLOCK, R, D_B), lambda b, j: (b, j, 0, 0))
    tw_blk = pl.BlockSpec((FFT_BLOCK, R, GROUP), lambda b, j: (j, 0, 0))
    shape4 = jax.ShapeDtypeStruct((n_seq, R, R, D_B), F32)
    br, bi = pl.pallas_call(
        _fourier_stage1_kernel,
        grid=(n_seq, R // FFT_BLOCK),
        in_specs=[fast_blk, fast_blk, pl.BlockSpec((2 * R, 2 * R), lambda b, j: (0, 0)), tw_blk, tw_blk],
        out_specs=(fast_blk, fast_blk),
        out_shape=(shape4, shape4),
        compiler_params=_cparams(2),
        name="fourier_stage1",
    )(zr.reshape(n_seq, R, R, D_B), zi.reshape(n_seq, R, R, D_B),
      jnp.asarray(m1, F32), jnp.asarray(tc), jnp.asarray(ts))

    y = pl.pallas_call(
        functools.partial(_fourier_stage2_kernel, scale),
        grid=(n_seq, R // FFT_BLOCK),
        in_specs=[slow_blk, slow_blk, pl.BlockSpec((R, 2 * R), lambda b, j: (0, 0))],
        out_specs=fast_blk,
        out_shape=shape4,
        compiler_params=_cparams(2),
        name="fourier_stage2",
    )(br, bi, jnp.asarray(cs, F32))
    return y.reshape(n_seq * seq_len, D_B)


def _merge_kernel(has_pe, final, *refs):
    refs = list(refs)
    ya_ref, yf_ref, zb_ref, ga_ref, gb_ref, x_ref = refs[:6]
    pos = 6
    x = x_ref[...]
    if has_pe:
        x = x + refs[pos][...]
        pos += 1
    mod_ref, wpa_ref, wpb_ref, wo_ref = refs[pos:pos + 4]
    pos += 4
    if final:
        fw_ref = refs[pos]
        pos += 1
    out_ref = refs[pos]
    yb = (yf_ref[...].astype(F32) * zb_ref[...].astype(F32)).astype(BF16)
    merged = (ga_ref[...].astype(F32) * _dot(ya_ref[...], wpa_ref[...])
              + gb_ref[...].astype(F32) * _dot(yb, wpb_ref[...]))
    out = _dot(merged.astype(BF16), wo_ref[...])
    xn = x + mod_ref[:, 2 * D:3 * D] * out
    if final:
        ms = jnp.mean(xn * xn, axis=-1, keepdims=True)
        xn = xn * lax.rsqrt(ms + EPS) * fw_ref[...]
    out_ref[...] = xn


def _merge(ya, yf, zb, pa, x, pe, mod, wpa, wpb, wo, final_w, layer, seq_len, latent):
    tokens = x.shape[0]
    tm = TM_MERGE
    has_pe = pe is not None
    final = final_w is not None
    row = lambda i: (i, 0)
    if latent:
        mod_idx = lambda i: (layer * MOD_ROWS + 1 + (i * tm) // seq_len, 0, 0)
    else:
        mod_idx = lambda i: (layer * MOD_ROWS, 0, 0)
    in_specs = [
        pl.BlockSpec((tm, D_A), row),
        pl.BlockSpec((tm, D_B), row),
        pl.BlockSpec((tm, D_B), row),
        pl.BlockSpec((tm, D), lambda i: (i, PA_GA)),
        pl.BlockSpec((tm, D), lambda i: (i, PA_GB)),
        pl.BlockSpec((tm, D), row),
    ]
    args = [ya, yf, zb, pa, pa, x]
    if has_pe:
        per_seq = seq_len // tm
        in_specs.append(pl.BlockSpec((tm, D), lambda i: (i % per_seq, 0)))
        args.append(pe)
    in_specs += [
        pl.BlockSpec((None, 1, 3 * D), mod_idx),
        pl.BlockSpec((None, D_A, D), lambda i: (layer, 0, 0)),
        pl.BlockSpec((None, D_B, D), lambda i: (layer, 0, 0)),
        pl.BlockSpec((None, D, D), lambda i: (layer, 0, 0)),
    ]
    args += [mod, wpa, wpb, wo]
    if final:
        in_specs.append(pl.BlockSpec((1, D), lambda i: (0, 0)))
        args.append(final_w)
    return pl.pallas_call(
        functools.partial(_merge_kernel, has_pe, final),
        grid=(tokens // tm,),
        in_specs=in_specs,
        out_specs=pl.BlockSpec((tm, D), row),
        out_shape=jax.ShapeDtypeStruct((tokens, D), F32),
        compiler_params=_cparams(1),
        name="merge",
    )(*args)


def kernel(x_prompt, x_sample, state_hgrn, c, c_ctx, norm_w, w_ada, b_ada, w_in, lb_raw,
           gnorm_w, w_pa, w_pb, w_o, final_norm_w):
    n_p, len_p, _ = x_prompt.shape
    n_s, len_s, _ = x_sample.shape

    cond = jnp.zeros((MOD_ROWS, D), F32).at[0].set(c_ctx).at[1:1 + n_s].set(c)
    mod = _modulation(cond, w_ada, b_ada)
    lbs = _lower_bounds(lb_raw)

    w_in_bf = w_in.astype(BF16)
    wpa_bf = w_pa.astype(BF16)
    wpb_bf = w_pb.astype(BF16)
    wo_bf = w_o.astype(BF16)
    norm_w3 = norm_w.reshape(DEPTH, 1, D)
    gnorm3 = gnorm_w.reshape(DEPTH, 1, D_A)
    final_w = final_norm_w.reshape(1, D)

    cc_np, sc_np = _dft_cos_sin(GROUP)
    cc = jnp.asarray(cc_np, F32)
    sc = jnp.asarray(sc_np, F32)
    pe = jnp.asarray(_pos_embed_table(len_s, D))

    xp = x_prompt.reshape(n_p * len_p, D)
    xs = x_sample.reshape(n_s * len_s, D)
    states = []
    for l in range(DEPTH):
        last = l == DEPTH - 1
        fw = final_w if last else None
        pe_l = pe if l == 0 else None

        pf, pa, zr, zi, zb = _proj(xp, None, mod, norm_w3, lbs, w_in_bf, cc, sc, l, len_p, False)
        ya, st = _scan(pf, pa, gnorm3, None, l, n_p, len_p, True)
        yf = _fourier_dense(zr, zi, n_p, len_p)
        xp = _merge(ya, yf, zb, pa, xp, None, mod, wpa_bf, wpb_bf, wo_bf, fw, l, len_p, False)
        states.append(st)

        pf, pa, zr, zi, zb = _proj(xs, pe_l, mod, norm_w3, lbs, w_in_bf, cc, sc, l, len_s, True)
        ya, _ = _scan(pf, pa, gnorm3, state_hgrn, l, n_s, len_s, False)
        yf = _fourier_two_stage(zr, zi, n_s, len_s)
        xs = _merge(ya, yf, zb, pa, xs, pe_l, mod, wpa_bf, wpb_bf, wo_bf, fw, l, len_s, True)

    y_prompt = xp.reshape(n_p, len_p, D)
    y_sample = xs.reshape(n_s, len_s, D)
    new_state = jnp.stack(states, axis=1)
    return (y_prompt, y_sample, new_state)
```

```python
import functools

import jax
import jax.numpy as jnp
import numpy as np
from jax import lax
from jax.experimental import pallas as pl
from jax.experimental.pallas import tpu as pltpu

D = 1024
DEPTH = 4
N_HEADS = 8
HEAD = 128
D_A = N_HEADS * HEAD
N_GROUPS = 4
GROUP = 128
D_B = N_GROUPS * GROUP
D_IN = 5 * D_A + 2 * D_B + 2 * D
GRID_W = 64
EPS = 1e-6

V7X_VMEM_LIMIT_BYTES = 56 * 1024 * 1024

SCAN_CHUNK = 64
SCAN_GROUP = 8
SCAN_UNROLL = 4
SCAN_SHORT_SEQ = 512
SCAN_HEADS_SHORT = 4
EXP_CLAMP = 80.0
TM_PROJ = 256
TM_MERGE = 512
FFT_RADIX = 64
FFT_BLOCK = 8
MOD_ROWS = 8

F32 = jnp.float32
BF16 = jnp.bfloat16


def _cparams(n_axes):
    return pltpu.CompilerParams(
        dimension_semantics=("arbitrary",) * n_axes,
        vmem_limit_bytes=V7X_VMEM_LIMIT_BYTES,
    )


def _dot(a, b):
    return jnp.dot(a, b, preferred_element_type=F32)


def _dot_nt(a, b):
    return lax.dot_general(a, b, (((1,), (1,)), ((), ())), preferred_element_type=F32)


def _dot_tn(a, b):
    return lax.dot_general(a, b, (((0,), (0,)), ((), ())), preferred_element_type=F32)


def _sigmoid(x):
    return 1.0 / (1.0 + jnp.exp(-x))


def _silu(x):
    return x * _sigmoid(x)


def _dft_cos_sin(n):
    k = np.arange(n, dtype=np.float64)
    ang = 2.0 * np.pi * np.outer(k, k) / n
    return np.cos(ang), np.sin(ang)


def _pos_embed_table(length, d):
    rows = length // GRID_W
    r = np.repeat(np.arange(rows, dtype=np.float64), GRID_W)
    col = np.tile(np.arange(GRID_W, dtype=np.float64), rows)
    nf = d // 4
    freqs = 1.0 / (10000.0 ** (np.arange(nf, dtype=np.float64) / nf))

    def emb(p):
        a = p[:, None] * freqs[None, :]
        return np.concatenate([np.sin(a), np.cos(a)], axis=-1)

    return np.concatenate([emb(r), emb(col)], axis=-1).astype(np.float32)


def _lb_kernel(raw_ref, out_ref):
    x = raw_ref[...]
    m = jnp.max(x, axis=0, keepdims=True)
    e = jnp.exp(x - m)
    p = e / jnp.sum(e, axis=0, keepdims=True)
    cs = p[0:1]
    first = cs
    out_ref[0:1, :] = cs - first
    for l in range(1, DEPTH):
        cs = cs + p[l:l + 1]
        out_ref[l:l + 1, :] = cs - first


def _lower_bounds(lb_raw):
    raw = lb_raw.reshape(DEPTH, 2 * D_A)
    out = pl.pallas_call(
        _lb_kernel,
        out_shape=jax.ShapeDtypeStruct((DEPTH, 2 * D_A), F32),
        name="lower_bounds",
    )(raw)
    return out.reshape(DEPTH * 2, 1, D_A)


def _mod_kernel(cond_ref, w_ref, b_ref, out_ref):
    a = _silu(cond_ref[...])
    out_ref[...] = jnp.dot(a, w_ref[...], preferred_element_type=F32,
                           precision=lax.Precision.HIGHEST) + b_ref[...]


def _modulation(cond, w_ada, b_ada):
    tn = 512
    out = pl.pallas_call(
        _mod_kernel,
        grid=(DEPTH, 3 * D // tn),
        in_specs=[
            pl.BlockSpec((MOD_ROWS, D), lambda l, j: (0, 0)),
            pl.BlockSpec((None, D, tn), lambda l, j: (l, 0, j)),
            pl.BlockSpec((None, 1, tn), lambda l, j: (l, 0, j)),
        ],
        out_specs=pl.BlockSpec((None, MOD_ROWS, tn), lambda l, j: (l, 0, j)),
        out_shape=jax.ShapeDtypeStruct((DEPTH, MOD_ROWS, 3 * D), F32),
        compiler_params=_cparams(2),
        name="modulation",
    )(cond, w_ada, b_ada.reshape(DEPTH, 1, 3 * D))
    return out.reshape(DEPTH * MOD_ROWS, 1, 3 * D)


_Q0, _FF0, _V0, _ZA0, _U0, _ZB0, _GA0, _GB0 = (
    0, D_A, 3 * D_A, 4 * D_A, 5 * D_A, 5 * D_A + D_B, 5 * D_A + 2 * D_B, 5 * D_A + 2 * D_B + D)
PA_Q, PA_V, PA_Z, PA_GA, PA_GB, PA_KF = 0, 1, 2, 3, 4, 5
PA_WIDTH = 7 * D


def _forget_gate(x, lb):
    f = lb + (1.0 - lb) * _sigmoid(x)
    log_f = jnp.where(f < 1e-30, jnp.minimum(x, 0.0), jnp.log(jnp.maximum(f, 1e-30)))
    return log_f, 1.0 - f


def _proj_kernel(has_pe, *refs):
    refs = list(refs)
    x = refs[0][...]
    pos = 1
    if has_pe:
        x = x + refs[pos][...]
        pos += 1
    mod_ref, nw_ref, lbf_ref, lbb_ref, w_ref, cc_ref, sc_ref = refs[pos:pos + 7]
    pf_ref, pa_ref, zr_ref, zi_ref, zb_ref = refs[pos + 7:]

    ms = jnp.mean(x * x, axis=-1, keepdims=True)
    y = x * lax.rsqrt(ms + EPS) * nw_ref[...]
    shift = mod_ref[:, 0:D]
    scale = mod_ref[:, D:2 * D]
    h = (y * (1.0 + scale) + shift).astype(BF16)

    def sec(c0, width):
        return _dot(h, w_ref[:, c0:c0 + width])

    def put(block, val):
        pa_ref[:, block * D:(block + 1) * D] = val.astype(BF16)

    tm = x.shape[0]
    row = lax.broadcasted_iota(jnp.int32, (tm, tm), 0)
    col = lax.broadcasted_iota(jnp.int32, (tm, tm), 1)
    same_chunk = (row // SCAN_CHUNK) == (col // SCAN_CHUNK)
    tris = (jnp.where(same_chunk & (col <= row), 1.0, 0.0).astype(BF16),
            jnp.where(same_chunk & (col >= row), 1.0, 0.0).astype(BF16))
    for d, lb_ref in enumerate((lbf_ref, lbb_ref)):
        g, kk = _forget_gate(sec(_FF0 + d * D_A, D_A), lb_ref[...])
        g_hi = g.astype(BF16)
        g_lo = (g - g_hi.astype(F32)).astype(BF16)
        pf_ref[:, d * D_A:(d + 1) * D_A] = _dot(tris[d], g_hi) + _dot(tris[d], g_lo)
        put(PA_KF + d, kk)

    put(PA_Q, _silu(sec(_Q0, D_A)))
    put(PA_V, sec(_V0, D_A))
    put(PA_Z, _silu(sec(_ZA0, D_A)))
    put(PA_GA, _sigmoid(sec(_GA0, D)))
    put(PA_GB, _sigmoid(sec(_GB0, D)))
    zb_ref[...] = _silu(sec(_ZB0, D_B)).astype(BF16)
    u = sec(_U0, D_B).astype(BF16)
    cc = cc_ref[...].astype(BF16)
    sc = sc_ref[...].astype(BF16)
    for g in range(N_GROUPS):
        ug = u[:, g * GROUP:(g + 1) * GROUP]
        zr_ref[:, g * GROUP:(g + 1) * GROUP] = _dot(ug, cc)
        zi_ref[:, g * GROUP:(g + 1) * GROUP] = -_dot(ug, sc)


def _proj(x, pe, mod, norm_w3, lbs, w_in_bf, cc, sc, layer, seq_len, latent):
    tokens = x.shape[0]
    tm = TM_PROJ
    has_pe = pe is not None
    per_seq = seq_len // tm
    if latent:
        mod_idx = lambda i: (layer * MOD_ROWS + 1 + i // per_seq, 0, 0)
    else:
        mod_idx = lambda i: (layer * MOD_ROWS, 0, 0)
    in_specs = [pl.BlockSpec((tm, D), lambda i: (i, 0))]
    args = [x]
    if has_pe:
        in_specs.append(pl.BlockSpec((tm, D), lambda i: (i % per_seq, 0)))
        args.append(pe)
    in_specs += [
        pl.BlockSpec((None, 1, 3 * D), mod_idx),
        pl.BlockSpec((None, 1, D), lambda i: (layer, 0, 0)),
        pl.BlockSpec((None, 1, D_A), lambda i: (2 * layer, 0, 0)),
        pl.BlockSpec((None, 1, D_A), lambda i: (2 * layer + 1, 0, 0)),
        pl.BlockSpec((None, D, D_IN), lambda i: (layer, 0, 0), pipeline_mode=pl.Buffered(1)),
        pl.BlockSpec((GROUP, GROUP), lambda i: (0, 0)),
        pl.BlockSpec((GROUP, GROUP), lambda i: (0, 0)),
    ]
    args += [mod, norm_w3, lbs, lbs, w_in_bf, cc, sc]
    out_shape = (
        jax.ShapeDtypeStruct((tokens, 2 * D_A), F32),
        jax.ShapeDtypeStruct((tokens, PA_WIDTH), BF16),
        jax.ShapeDtypeStruct((tokens, D_B), F32),
        jax.ShapeDtypeStruct((tokens, D_B), F32),
        jax.ShapeDtypeStruct((tokens, D_B), BF16),
    )
    out_specs = (
        pl.BlockSpec((tm, 2 * D_A), lambda i: (i, 0)),
        pl.BlockSpec((tm, PA_WIDTH), lambda i: (i, 0)),
        pl.BlockSpec((tm, D_B), lambda i: (i, 0)),
        pl.BlockSpec((tm, D_B), lambda i: (i, 0)),
        pl.BlockSpec((tm, D_B), lambda i: (i, 0)),
    )
    return pl.pallas_call(
        functools.partial(_proj_kernel, has_pe),
        grid=(tokens // tm,),
        in_specs=in_specs,
        out_specs=out_specs,
        out_shape=out_shape,
        compiler_params=_cparams(1),
        name="proj",
    )(*args)


def _scan_kernel(seq_len, heads, has_init, emit_state, has_acc, *refs):
    refs = list(refs)
    bf_ref, bb_ref, q_ref, v_ref, z_ref, kf_ref, kb_ref, gn_ref = refs[:8]
    pos = 8
    init_ref = None
    if has_init:
        init_ref = refs[pos]
        pos += 1
    if has_acc:
        pos += 1
    y_ref = refs[pos]
    pos += 1
    st_ref = None
    if emit_state:
        st_ref = refs[pos]
        pos += 1
    (o_ref, sf_ref, sb_ref, qinf_ref, qinb_ref, uf_ref, ub_ref, elf_ref, elb_ref) = refs[pos:pos + 9]

    C = SCAN_CHUNK
    H = C // 2
    n_chunks = seq_len // C
    G = min(SCAN_GROUP, n_chunks)
    GC = G * C
    n_groups = seq_len // GC
    lanes = [slice(hd * HEAD, (hd + 1) * HEAD) for hd in range(heads)]

    def pair_masks(n):
        rr = lax.broadcasted_iota(jnp.int32, (n, n), 0)
        cc = lax.broadcasted_iota(jnp.int32, (n, n), 1)
        return cc <= rr, cc >= rr

    mask_c = pair_masks(C)
    mask_h = pair_masks(H)
    dirs = (
        (bf_ref, kf_ref, C - 1, qinf_ref, uf_ref, elf_ref),
        (bb_ref, kb_ref, 0, qinb_ref, ub_ref, elb_ref),
    )

    def split_factors(bq, qv, kv):
        n = bq.shape[0]
        r = 0.5 * (bq[0:1, :] + bq[n - 1:n, :])
        e = jnp.exp(jnp.clip(bq - r, -EXP_CLAMP, EXP_CLAMP))
        return qv * e, kv * (1.0 / e), r, jnp.abs(bq[0:1, :] - r)

    def halves(d, bc, qc, kc, vc):
        first, second = (slice(0, H), slice(H, C)) if d == 0 else (slice(H, C), slice(0, H))
        edge = bc[H - 1:H, :] if d == 0 else bc[H:H + 1, :]
        atts, dev = [], None
        for rows in (first, second):
            qt, kt, _, dev_h = split_factors(bc[rows], qc[rows], kc[rows])
            att = _dot_nt(qt.astype(BF16), kt.astype(BF16))
            atts.append(jnp.where(mask_h[d], att, 0.0).astype(BF16))
            dev = dev_h if dev is None else jnp.maximum(dev, dev_h)
        cross = _dot_nt((qc[second] * jnp.exp(bc[second] - edge)).astype(BF16),
                        (kc[first] * jnp.exp(edge - bc[first])).astype(BF16)).astype(BF16)
        o_first = _dot(atts[0], vc[first])
        o_second = _dot(atts[1], vc[second]) + _dot(cross, vc[first])
        parts = [o_first, o_second] if d == 0 else [o_second, o_first]
        return jnp.concatenate(parts, axis=0), dev

    def pairwise(d, bc, qc, kc, vf):
        s_idx = lax.broadcasted_iota(jnp.int32, (C, 1), 0)

        def row(t, o_blk):
            sel = s_idx == t
            b_t = jnp.sum(jnp.where(sel, bc, 0.0), axis=0, keepdims=True)
            q_t = jnp.sum(jnp.where(sel, qc, 0.0), axis=0, keepdims=True)
            allowed = (s_idx <= t) if d == 0 else (s_idx >= t)
            decay = jnp.where(allowed, jnp.exp(jnp.where(allowed, b_t - bc, 0.0)), 0.0)
            att_t = jnp.sum(decay * (q_t * kc), axis=1, keepdims=True)
            o_t = jnp.sum(att_t * vf, axis=0, keepdims=True)
            return jnp.where(sel, o_t, o_blk)

        return lax.fori_loop(0, C, row, jnp.zeros((C, HEAD), F32))

    def redo_chunk(hd, chunk, rows):
        ls = lanes[hd]
        bcs = [b_ref[rows, ls] for b_ref, _, _, _, _, _ in dirs]
        span = None
        for bc in bcs:
            r = 0.5 * (bc[0:1, :] + bc[C - 1:C, :])
            dev = jnp.maximum(jnp.abs(bc[0:1, :] - r), jnp.abs(r))
            span = dev if span is None else jnp.maximum(span, dev)

        @pl.when(jnp.max(span) > EXP_CLAMP)
        def _():
            qc = q_ref[rows, ls].astype(F32)
            vc = v_ref[rows, ls]
            kcs = [k_ref[rows, ls].astype(F32) for _, k_ref, _, _, _, _ in dirs]
            o_sum, span_h = None, None
            for d, (_, _, last_row, qin_ref, u_ref, _) in enumerate(dirs):
                bc, kc = bcs[d], kcs[d]
                b_last = bc[last_row:last_row + 1, :]
                qin_ref[rows, ls] = (qc * jnp.exp(bc)).astype(BF16)
                u_ref[hd * n_chunks + chunk] = _dot_tn(vc, (kc * jnp.exp(b_last - bc)).astype(BF16))
                o_d, dev = halves(d, bc, qc, kc, vc)
                o_sum = o_d if o_sum is None else o_sum + o_d
                span_h = dev if span_h is None else jnp.maximum(span_h, dev)
            o_ref[rows, ls] = o_sum

            @pl.when(jnp.max(span_h) > EXP_CLAMP)
            def _():
                vf = vc.astype(F32)
                o_ref[rows, ls] = pairwise(0, bcs[0], qc, kcs[0], vf) + pairwise(1, bcs[1], qc, kcs[1], vf)

    def local(j, carry):
        r0 = pl.multiple_of(j * GC, GC)
        work = []
        span = None
        for hd, ls in enumerate(lanes):
            q = q_ref[pl.ds(r0, GC), ls].astype(F32)
            v = v_ref[pl.ds(r0, GC), ls]
            for d, (b_ref, k_ref, last_row, qin_ref, u_ref, el_ref) in enumerate(dirs):
                b = b_ref[pl.ds(r0, GC), ls]
                kk = k_ref[pl.ds(r0, GC), ls].astype(F32)
                for c in range(G):
                    sl = slice(c * C, (c + 1) * C)
                    bc = b[sl]
                    b_last = bc[last_row:last_row + 1, :]
                    qt, kt, r, dev = split_factors(bc, q[sl], kk[sl])
                    dev = jnp.maximum(dev, jnp.abs(r))
                    span = dev if span is None else jnp.maximum(span, dev)
                    qin_ref[pl.ds(r0 + c * C, C), ls] = (qt * jnp.exp(r)).astype(BF16)
                    el_ref[hd * n_chunks + j * G + c] = jnp.exp(b_last)
                    k_out = (kt * jnp.exp(b_last - r)).astype(BF16)
                    work.append((qt.astype(BF16), kt.astype(BF16), k_out, mask_c[d], v[sl], u_ref,
                                 hd * n_chunks + j * G + c))
        atts = [_dot_nt(qt, kt) for qt, kt, _, _, _, _, _ in work]
        atts = [jnp.where(w[3], a, 0.0).astype(BF16) for w, a in zip(work, atts)]
        for (_, _, k_out, _, vc, u_ref, slot) in work:
            u_ref[slot] = _dot_tn(vc, k_out)
        outs = [_dot(a, w[4]) for w, a in zip(work, atts)]
        for hd, ls in enumerate(lanes):
            o_f = jnp.concatenate(outs[(2 * hd) * G:(2 * hd + 1) * G], axis=0)
            o_b = jnp.concatenate(outs[(2 * hd + 1) * G:(2 * hd + 2) * G], axis=0)
            o_ref[pl.ds(r0, GC), ls] = o_f + o_b

        @pl.when(jnp.max(span) > EXP_CLAMP)
        def _():
            for hd in range(heads):
                def redo(c, carry, hd=hd):
                    redo_chunk(hd, j * G + c, pl.ds(pl.multiple_of(j * GC + c * C, C), C))
                    return carry

                lax.fori_loop(0, G, redo, 0)

        return carry

    lax.fori_loop(0, n_groups, local, 0)

    for hd in range(heads):
        base = hd * n_chunks
        if has_init:
            s_init = (init_ref[0, hd].T, init_ref[1, hd].T)
        else:
            s_init = (jnp.zeros((HEAD, HEAD), F32), jnp.zeros((HEAD, HEAD), F32))

        def step(i, carry, base=base):
            s_f, s_b = carry
            kf = base + i
            kb = base + n_chunks - 1 - i
            sf_ref[kf] = s_f.astype(BF16)
            sb_ref[kb] = s_b.astype(BF16)
            return (s_f * elf_ref[kf] + uf_ref[kf], s_b * elb_ref[kb] + ub_ref[kb])

        s_f, s_b = lax.fori_loop(0, n_chunks, step, s_init, unroll=SCAN_UNROLL)
        if emit_state:
            st_ref[0, hd] = s_f.T
            st_ref[1, hd] = s_b.T

    def add_state_readout(j):
        r0 = pl.multiple_of(j * GC, GC)
        for hd, ls in enumerate(lanes):
            o = o_ref[pl.ds(r0, GC), ls]
            for qin_ref, s_ref in ((qinf_ref, sf_ref), (qinb_ref, sb_ref)):
                o = o + jnp.concatenate(
                    [_dot_nt(qin_ref[pl.ds(r0 + c * C, C), ls], s_ref[hd * n_chunks + j * G + c])
                     for c in range(G)], axis=0)
            o_ref[pl.ds(r0, GC), ls] = o

    def normalise(j):
        r0 = pl.multiple_of(j * GC, GC)
        for ls in lanes:
            o = o_ref[pl.ds(r0, GC), ls]
            ms = jnp.mean(o * o, axis=-1, keepdims=True)
            o = o * lax.rsqrt(ms + EPS) * gn_ref[:, ls]
            y_ref[pl.ds(r0, GC), ls] = (o * z_ref[pl.ds(r0, GC), ls].astype(F32)).astype(BF16)

    add_state_readout(0)

    def finish(j, carry):
        normalise(j - 1)
        add_state_readout(j)
        return carry

    lax.fori_loop(1, n_groups, finish, 0)
    normalise(n_groups - 1)


def _scan(pf, pa, gnorm3, init_state, layer, n_seq, seq_len, emit_state, state_acc=None):
    tokens = n_seq * seq_len
    n_chunks = seq_len // SCAN_CHUNK
    has_init = init_state is not None
    has_acc = state_acc is not None
    heads = SCAN_HEADS_SHORT if seq_len <= SCAN_SHORT_SEQ else 1
    width = heads * HEAD
    blk = lambda c0: pl.BlockSpec((seq_len, width), lambda b, h, c0=c0: (b, c0 // heads + h))
    in_specs = [
        blk(0), blk(N_HEADS),
        blk(PA_Q * N_HEADS), blk(PA_V * N_HEADS), blk(PA_Z * N_HEADS),
        blk(PA_KF * N_HEADS), blk((PA_KF + 1) * N_HEADS),
        pl.BlockSpec((None, 1, width), lambda b, h: (layer, 0, h)),
    ]
    args = [pf, pf, pa, pa, pa, pa, pa, gnorm3]
    if has_init:
        in_specs.append(pl.BlockSpec((None, None, 2, heads, HEAD, HEAD),
                                     lambda b, h: (b, layer, 0, h, 0, 0)))
        args.append(init_state)
    aliases = {}
    if has_acc:
        aliases = {len(args): 1}
        in_specs.append(pl.BlockSpec(memory_space=pl.ANY))
        args.append(state_acc)
    out_shape = [jax.ShapeDtypeStruct((tokens, D_A), BF16)]
    out_specs = [pl.BlockSpec((seq_len, width), lambda b, h: (b, h))]
    if emit_state:
        out_shape.append(jax.ShapeDtypeStruct((n_seq, DEPTH, 2, N_HEADS, HEAD, HEAD), F32))
        out_specs.append(pl.BlockSpec((None, None, 2, heads, HEAD, HEAD),
                                      lambda b, h: (b, layer, 0, h, 0, 0)))
    slots = heads * n_chunks
    res = pl.pallas_call(
        functools.partial(_scan_kernel, seq_len, heads, has_init, emit_state, has_acc),
        grid=(n_seq, N_HEADS // heads),
        in_specs=in_specs,
        out_specs=out_specs,
        out_shape=out_shape,
        input_output_aliases=aliases,
        scratch_shapes=[
            pltpu.VMEM((seq_len, width), F32),
            pltpu.VMEM((slots, HEAD, HEAD), BF16), pltpu.VMEM((slots, HEAD, HEAD), BF16),
            pltpu.VMEM((seq_len, width), BF16), pltpu.VMEM((seq_len, width), BF16),
            pltpu.VMEM((slots, HEAD, HEAD), F32), pltpu.VMEM((slots, HEAD, HEAD), F32),
            pltpu.VMEM((slots, 1, HEAD), F32), pltpu.VMEM((slots, 1, HEAD), F32),
        ],
        compiler_params=_cparams(2),
        name="scan",
    )(*args)
    return res if emit_state else (res[0], None)


def _fourier_dense_kernel(scale, zr_ref, zi_ref, c_ref, s_ref, y_ref):
    yf = (_dot(c_ref[...].astype(BF16), zr_ref[...].astype(BF16))
          + _dot(s_ref[...].astype(BF16), zi_ref[...].astype(BF16)))
    y_ref[...] = (yf * scale).astype(BF16)


def _fourier_dense(zr, zi, n_seq, seq_len):
    c, s = _dft_cos_sin(seq_len)
    scale = 1.0 / np.sqrt(seq_len * GROUP)
    blk = pl.BlockSpec((seq_len, D_B), lambda b: (b, 0))
    mat = pl.BlockSpec((seq_len, seq_len), lambda b: (0, 0))
    return pl.pallas_call(
        functools.partial(_fourier_dense_kernel, scale),
        grid=(n_seq,),
        in_specs=[blk, blk, mat, mat],
        out_specs=blk,
        out_shape=jax.ShapeDtypeStruct((n_seq * seq_len, D_B), BF16),
        compiler_params=_cparams(1),
        name="fourier_dense",
    )(zr, zi, jnp.asarray(c, F32), jnp.asarray(s, F32))


def _fourier_stage1_kernel(zr_ref, zi_ref, m_ref, tc_ref, ts_ref, br_ref, bi_ref):
    m = m_ref[...].astype(BF16)
    for n in range(FFT_BLOCK):
        z = jnp.concatenate([zr_ref[:, n, :], zi_ref[:, n, :]], axis=0).astype(BF16)
        a = _dot(m, z)
        ar = a[:FFT_RADIX]
        ai = a[FFT_RADIX:]
        tc = jnp.concatenate([tc_ref[n]] * N_GROUPS, axis=1)
        ts = jnp.concatenate([ts_ref[n]] * N_GROUPS, axis=1)
        br_ref[:, n, :] = ar * tc + ai * ts
        bi_ref[:, n, :] = ai * tc - ar * ts


def _fourier_stage2_kernel(scale, br_ref, bi_ref, cs_ref, y_ref):
    cs = cs_ref[...].astype(BF16)
    for j in range(FFT_BLOCK):
        bcat = jnp.concatenate([br_ref[j], bi_ref[j]], axis=0).astype(BF16)
        y_ref[:, j, :] = _dot(cs, bcat) * scale


def _fourier_two_stage(zr, zi, n_seq, seq_len):
    R = FFT_RADIX
    assert seq_len == R * R
    c, s = _dft_cos_sin(R)
    m1 = np.block([[c, s], [-s, c]])
    idx = np.arange(R, dtype=np.float64)
    ang = 2.0 * np.pi * np.outer(idx, idx) / seq_len
    tc = np.repeat(np.cos(ang)[:, :, None], GROUP, axis=2).astype(np.float32)
    ts = np.repeat(np.sin(ang)[:, :, None], GROUP, axis=2).astype(np.float32)
    cs = np.concatenate([c, s], axis=1)
    scale = 1.0 / np.sqrt(seq_len * GROUP)

    fast_blk = pl.BlockSpec((None, R, FFT_BLOCK, D_B), lambda b, j: (b, 0, j, 0))
    slow_blk = pl.BlockSpec((None, FFT_BLOCK, R, D_B), lambda b, j: (b, j, 0, 0))
    tw_blk = pl.BlockSpec((FFT_BLOCK, R, GROUP), lambda b, j: (j, 0, 0))
    shape4 = jax.ShapeDtypeStruct((n_seq, R, R, D_B), F32)
    br, bi = pl.pallas_call(
        _fourier_stage1_kernel,
        grid=(n_seq, R // FFT_BLOCK),
        in_specs=[fast_blk, fast_blk, pl.BlockSpec((2 * R, 2 * R), lambda b, j: (0, 0)), tw_blk, tw_blk],
        out_specs=(fast_blk, fast_blk),
        out_shape=(shape4, shape4),
        compiler_params=_cparams(2),
        name="fourier_stage1",
    )(zr.reshape(n_seq, R, R, D_B), zi.reshape(n_seq, R, R, D_B),
      jnp.asarray(m1, F32), jnp.asarray(tc), jnp.asarray(ts))

    y = pl.pallas_call(
        functools.partial(_fourier_stage2_kernel, scale),
        grid=(n_seq, R // FFT_BLOCK),
        in_specs=[slow_blk, slow_blk, pl.BlockSpec((R, 2 * R), lambda b, j: (0, 0))],
        out_specs=fast_blk,
        out_shape=shape4,
        compiler_params=_cparams(2),
        name="fourier_stage2",
    )(br, bi, jnp.asarray(cs, F32))
    return y.reshape(n_seq * seq_len, D_B)


def _merge_kernel(has_pe, final, *refs):
    refs = list(refs)
    ya_ref, yf_ref, zb_ref, ga_ref, gb_ref, x_ref = refs[:6]
    pos = 6
    x = x_ref[...]
    if has_pe:
        x = x + refs[pos][...]
        pos += 1
    mod_ref, wpa_ref, wpb_ref, wo_ref = refs[pos:pos + 4]
    pos += 4
    if final:
        fw_ref = refs[pos]
        pos += 1
    out_ref = refs[pos]
    yb = (yf_ref[...].astype(F32) * zb_ref[...].astype(F32)).astype(BF16)
    merged = (ga_ref[...].astype(F32) * _dot(ya_ref[...], wpa_ref[...])
              + gb_ref[...].astype(F32) * _dot(yb, wpb_ref[...]))
    out = _dot(merged.astype(BF16), wo_ref[...])
    xn = x + mod_ref[:, 2 * D:3 * D] * out
    if final:
        ms = jnp.mean(xn * xn, axis=-1, keepdims=True)
        xn = xn * lax.rsqrt(ms + EPS) * fw_ref[...]
    out_ref[...] = xn


def _merge(ya, yf, zb, pa, x, pe, mod, wpa, wpb, wo, final_w, layer, seq_len, latent):
    tokens = x.shape[0]
    tm = TM_MERGE
    has_pe = pe is not None
    final = final_w is not None
    row = lambda i: (i, 0)
    if latent:
        mod_idx = lambda i: (layer * MOD_ROWS + 1 + (i * tm) // seq_len, 0, 0)
    else:
        mod_idx = lambda i: (layer * MOD_ROWS, 0, 0)
    in_specs = [
        pl.BlockSpec((tm, D_A), row),
        pl.BlockSpec((tm, D_B), row),
        pl.BlockSpec((tm, D_B), row),
        pl.BlockSpec((tm, D), lambda i: (i, PA_GA)),
        pl.BlockSpec((tm, D), lambda i: (i, PA_GB)),
        pl.BlockSpec((tm, D), row),
    ]
    args = [ya, yf, zb, pa, pa, x]
    if has_pe:
        per_seq = seq_len // tm
        in_specs.append(pl.BlockSpec((tm, D), lambda i: (i % per_seq, 0)))
        args.append(pe)
    in_specs += [
        pl.BlockSpec((None, 1, 3 * D), mod_idx),
        pl.BlockSpec((None, D_A, D), lambda i: (layer, 0, 0)),
        pl.BlockSpec((None, D_B, D), lambda i: (layer, 0, 0)),
        pl.BlockSpec((None, D, D), lambda i: (layer, 0, 0)),
    ]
    args += [mod, wpa, wpb, wo]
    if final:
        in_specs.append(pl.BlockSpec((1, D), lambda i: (0, 0)))
        args.append(final_w)
    return pl.pallas_call(
        functools.partial(_merge_kernel, has_pe, final),
        grid=(tokens // tm,),
        in_specs=in_specs,
        out_specs=pl.BlockSpec((tm, D), row),
        out_shape=jax.ShapeDtypeStruct((tokens, D), F32),
        compiler_params=_cparams(1),
        name="merge",
    )(*args)


def kernel(x_prompt, x_sample, state_hgrn, c, c_ctx, norm_w, w_ada, b_ada, w_in, lb_raw,
           gnorm_w, w_pa, w_pb, w_o, final_norm_w):
    n_p, len_p, _ = x_prompt.shape
    n_s, len_s, _ = x_sample.shape

    cond = jnp.zeros((MOD_ROWS, D), F32).at[0].set(c_ctx).at[1:1 + n_s].set(c)
    mod = _modulation(cond, w_ada, b_ada)
    lbs = _lower_bounds(lb_raw)

    w_in_bf = w_in.astype(BF16)
    wpa_bf = w_pa.astype(BF16)
    wpb_bf = w_pb.astype(BF16)
    wo_bf = w_o.astype(BF16)
    norm_w3 = norm_w.reshape(DEPTH, 1, D)
    gnorm3 = gnorm_w.reshape(DEPTH, 1, D_A)
    final_w = final_norm_w.reshape(1, D)

    cc_np, sc_np = _dft_cos_sin(GROUP)
    cc = jnp.asarray(cc_np, F32)
    sc = jnp.asarray(sc_np, F32)
    pe = jnp.asarray(_pos_embed_table(len_s, D))

    xp = x_prompt.reshape(n_p * len_p, D)
    xs = x_sample.reshape(n_s * len_s, D)
    new_state = None
    for l in range(DEPTH):
        last = l == DEPTH - 1
        fw = final_w if last else None
        pe_l = pe if l == 0 else None

        pf, pa, zr, zi, zb = _proj(xp, None, mod, norm_w3, lbs, w_in_bf, cc, sc, l, len_p, False)
        ya, new_state = _scan(pf, pa, gnorm3, None, l, n_p, len_p, True, new_state)
        yf = _fourier_dense(zr, zi, n_p, len_p)
        xp = _merge(ya, yf, zb, pa, xp, None, mod, wpa_bf, wpb_bf, wo_bf, fw, l, len_p, False)

        pf, pa, zr, zi, zb = _proj(xs, pe_l, mod, norm_w3, lbs, w_in_bf, cc, sc, l, len_s, True)
        ya, _ = _scan(pf, pa, gnorm3, state_hgrn, l, n_s, len_s, False)
        yf = _fourier_two_stage(zr, zi, n_s, len_s)
        xs = _merge(ya, yf, zb, pa, xs, pe_l, mod, wpa_bf, wpb_bf, wo_bf, fw, l, len_s, True)

    y_prompt = xp.reshape(n_p, len_p, D)
    y_sample = xs.reshape(n_s, len_s, D)
    return (y_prompt, y_sample, new_state)
```

```python
import functools

import jax
import jax.numpy as jnp
import numpy as np
from jax import lax
from jax.experimental import pallas as pl
from jax.experimental.pallas import tpu as pltpu

D = 1024
DEPTH = 4
N_HEADS = 8
HEAD = 128
D_A = N_HEADS * HEAD
N_GROUPS = 4
GROUP = 128
D_B = N_GROUPS * GROUP
D_IN = 5 * D_A + 2 * D_B + 2 * D
GRID_W = 64
EPS = 1e-6

V7X_VMEM_LIMIT_BYTES = 56 * 1024 * 1024

SCAN_CHUNK = 64
SCAN_GROUP = 8
SCAN_UNROLL = 4
SCAN_SHORT_SEQ = 512
SCAN_HEADS_SHORT = 4
EXP_CLAMP = 80.0
TM_PROJ = 512
CUM_ROWS = 256
TM_MERGE = 512
FFT_RADIX = 64
FFT_BLOCK = 16
MOD_ROWS = 8

F32 = jnp.float32
BF16 = jnp.bfloat16


def _cparams(n_axes):
    return pltpu.CompilerParams(
        dimension_semantics=("arbitrary",) * n_axes,
        vmem_limit_bytes=V7X_VMEM_LIMIT_BYTES,
    )


def _dot(a, b):
    return jnp.dot(a, b, preferred_element_type=F32)


def _dot_nt(a, b):
    return lax.dot_general(a, b, (((1,), (1,)), ((), ())), preferred_element_type=F32)


def _dot_tn(a, b):
    return lax.dot_general(a, b, (((0,), (0,)), ((), ())), preferred_element_type=F32)


def _sigmoid(x):
    return 1.0 / (1.0 + jnp.exp(-x))


def _silu(x):
    return x * _sigmoid(x)


def _dft_cos_sin(n):
    k = np.arange(n, dtype=np.float64)
    ang = 2.0 * np.pi * np.outer(k, k) / n
    return np.cos(ang), np.sin(ang)


def _pos_embed_table(length, d):
    rows = length // GRID_W
    r = np.repeat(np.arange(rows, dtype=np.float64), GRID_W)
    col = np.tile(np.arange(GRID_W, dtype=np.float64), rows)
    nf = d // 4
    freqs = 1.0 / (10000.0 ** (np.arange(nf, dtype=np.float64) / nf))

    def emb(p):
        a = p[:, None] * freqs[None, :]
        return np.concatenate([np.sin(a), np.cos(a)], axis=-1)

    return np.concatenate([emb(r), emb(col)], axis=-1).astype(np.float32)


def _lb_kernel(raw_ref, out_ref):
    x = raw_ref[...]
    m = jnp.max(x, axis=0, keepdims=True)
    e = jnp.exp(x - m)
    p = e / jnp.sum(e, axis=0, keepdims=True)
    cs = p[0:1]
    first = cs
    out_ref[0:1, :] = cs - first
    for l in range(1, DEPTH):
        cs = cs + p[l:l + 1]
        out_ref[l:l + 1, :] = cs - first


def _lower_bounds(lb_raw):
    raw = lb_raw.reshape(DEPTH, 2 * D_A)
    out = pl.pallas_call(
        _lb_kernel,
        out_shape=jax.ShapeDtypeStruct((DEPTH, 2 * D_A), F32),
        name="lower_bounds",
    )(raw)
    return out.reshape(DEPTH * 2, 1, D_A)


def _mod_kernel(cond_ref, w_ref, b_ref, out_ref):
    a = _silu(cond_ref[...])
    out_ref[...] = jnp.dot(a, w_ref[...], preferred_element_type=F32,
                           precision=lax.Precision.HIGHEST) + b_ref[...]


def _modulation(cond, w_ada, b_ada):
    tn = 512
    out = pl.pallas_call(
        _mod_kernel,
        grid=(DEPTH, 3 * D // tn),
        in_specs=[
            pl.BlockSpec((MOD_ROWS, D), lambda l, j: (0, 0)),
            pl.BlockSpec((None, D, tn), lambda l, j: (l, 0, j)),
            pl.BlockSpec((None, 1, tn), lambda l, j: (l, 0, j)),
        ],
        out_specs=pl.BlockSpec((None, MOD_ROWS, tn), lambda l, j: (l, 0, j)),
        out_shape=jax.ShapeDtypeStruct((DEPTH, MOD_ROWS, 3 * D), F32),
        compiler_params=_cparams(2),
        name="modulation",
    )(cond, w_ada, b_ada.reshape(DEPTH, 1, 3 * D))
    return out.reshape(DEPTH * MOD_ROWS, 1, 3 * D)


_Q0, _FF0, _V0, _ZA0, _U0, _ZB0, _GA0, _GB0 = (
    0, D_A, 3 * D_A, 4 * D_A, 5 * D_A, 5 * D_A + D_B, 5 * D_A + 2 * D_B, 5 * D_A + 2 * D_B + D)
PA_Q, PA_V, PA_Z, PA_GA, PA_GB, PA_KF = 0, 1, 2, 3, 4, 5
PA_WIDTH = 7 * D


def _forget_gate(x, lb):
    f = lb + (1.0 - lb) * _sigmoid(x)
    log_f = jnp.where(f < 1e-30, jnp.minimum(x, 0.0), jnp.log(jnp.maximum(f, 1e-30)))
    return log_f, 1.0 - f


def _proj_kernel(has_pe, *refs):
    refs = list(refs)
    x = refs[0][...]
    pos = 1
    if has_pe:
        x = x + refs[pos][...]
        pos += 1
    mod_ref, nw_ref, lbf_ref, lbb_ref, w_ref, cc_ref, sc_ref = refs[pos:pos + 7]
    pf_ref, pa_ref, zr_ref, zi_ref, zb_ref = refs[pos + 7:]

    ms = jnp.mean(x * x, axis=-1, keepdims=True)
    y = x * lax.rsqrt(ms + EPS) * nw_ref[...]
    shift = mod_ref[:, 0:D]
    scale = mod_ref[:, D:2 * D]
    h = (y * (1.0 + scale) + shift).astype(BF16)

    def sec(c0, width):
        return _dot(h, w_ref[:, c0:c0 + width])

    def put(block, val):
        pa_ref[:, block * D:(block + 1) * D] = val.astype(BF16)

    row = lax.broadcasted_iota(jnp.int32, (CUM_ROWS, CUM_ROWS), 0)
    col = lax.broadcasted_iota(jnp.int32, (CUM_ROWS, CUM_ROWS), 1)
    same_chunk = (row // SCAN_CHUNK) == (col // SCAN_CHUNK)
    tris = (jnp.where(same_chunk & (col <= row), 1.0, 0.0).astype(BF16),
            jnp.where(same_chunk & (col >= row), 1.0, 0.0).astype(BF16))
    for d, lb_ref in enumerate((lbf_ref, lbb_ref)):
        g, kk = _forget_gate(sec(_FF0 + d * D_A, D_A), lb_ref[...])
        g_hi = g.astype(BF16)
        g_lo = (g - g_hi.astype(F32)).astype(BF16)
        for r0 in range(0, x.shape[0], CUM_ROWS):
            rows = slice(r0, r0 + CUM_ROWS)
            pf_ref[rows, d * D_A:(d + 1) * D_A] = _dot(tris[d], g_hi[rows]) + _dot(tris[d], g_lo[rows])
        put(PA_KF + d, kk)

    put(PA_Q, _silu(sec(_Q0, D_A)))
    put(PA_V, sec(_V0, D_A))
    put(PA_Z, _silu(sec(_ZA0, D_A)))
    put(PA_GA, _sigmoid(sec(_GA0, D)))
    put(PA_GB, _sigmoid(sec(_GB0, D)))
    zb_ref[...] = _silu(sec(_ZB0, D_B)).astype(BF16)
    u = sec(_U0, D_B).astype(BF16)
    cc = cc_ref[...].astype(BF16)
    sc = sc_ref[...].astype(BF16)
    for g in range(N_GROUPS):
        ug = u[:, g * GROUP:(g + 1) * GROUP]
        zr_ref[:, g * GROUP:(g + 1) * GROUP] = _dot(ug, cc)
        zi_ref[:, g * GROUP:(g + 1) * GROUP] = -_dot(ug, sc)


def _proj(x, pe, mod, norm_w3, lbs, w_in_bf, cc, sc, layer, seq_len, latent):
    tokens = x.shape[0]
    tm = TM_PROJ
    has_pe = pe is not None
    per_seq = seq_len // tm
    if latent:
        mod_idx = lambda i: (layer * MOD_ROWS + 1 + i // per_seq, 0, 0)
    else:
        mod_idx = lambda i: (layer * MOD_ROWS, 0, 0)
    in_specs = [pl.BlockSpec((tm, D), lambda i: (i, 0))]
    args = [x]
    if has_pe:
        in_specs.append(pl.BlockSpec((tm, D), lambda i: (i % per_seq, 0)))
        args.append(pe)
    in_specs += [
        pl.BlockSpec((None, 1, 3 * D), mod_idx),
        pl.BlockSpec((None, 1, D), lambda i: (layer, 0, 0)),
        pl.BlockSpec((None, 1, D_A), lambda i: (2 * layer, 0, 0)),
        pl.BlockSpec((None, 1, D_A), lambda i: (2 * layer + 1, 0, 0)),
        pl.BlockSpec((None, D, D_IN), lambda i: (layer, 0, 0), pipeline_mode=pl.Buffered(1)),
        pl.BlockSpec((GROUP, GROUP), lambda i: (0, 0)),
        pl.BlockSpec((GROUP, GROUP), lambda i: (0, 0)),
    ]
    args += [mod, norm_w3, lbs, lbs, w_in_bf, cc, sc]
    out_shape = (
        jax.ShapeDtypeStruct((tokens, 2 * D_A), F32),
        jax.ShapeDtypeStruct((tokens, PA_WIDTH), BF16),
        jax.ShapeDtypeStruct((tokens, D_B), F32),
        jax.ShapeDtypeStruct((tokens, D_B), F32),
        jax.ShapeDtypeStruct((tokens, D_B), BF16),
    )
    out_specs = (
        pl.BlockSpec((tm, 2 * D_A), lambda i: (i, 0)),
        pl.BlockSpec((tm, PA_WIDTH), lambda i: (i, 0)),
        pl.BlockSpec((tm, D_B), lambda i: (i, 0)),
        pl.BlockSpec((tm, D_B), lambda i: (i, 0)),
        pl.BlockSpec((tm, D_B), lambda i: (i, 0)),
    )
    return pl.pallas_call(
        functools.partial(_proj_kernel, has_pe),
        grid=(tokens // tm,),
        in_specs=in_specs,
        out_specs=out_specs,
        out_shape=out_shape,
        compiler_params=_cparams(1),
        name="proj",
    )(*args)


def _scan_kernel(seq_len, heads, has_init, emit_state, has_acc, *refs):
    refs = list(refs)
    bf_ref, bb_ref, q_ref, v_ref, z_ref, kf_ref, kb_ref, gn_ref = refs[:8]
    pos = 8
    init_ref = None
    if has_init:
        init_ref = refs[pos]
        pos += 1
    if has_acc:
        pos += 1
    y_ref = refs[pos]
    pos += 1
    st_ref = None
    if emit_state:
        st_ref = refs[pos]
        pos += 1
    (o_ref, sf_ref, sb_ref, qinf_ref, qinb_ref, uf_ref, ub_ref, elf_ref, elb_ref) = refs[pos:pos + 9]

    C = SCAN_CHUNK
    H = C // 2
    n_chunks = seq_len // C
    G = min(SCAN_GROUP, n_chunks)
    GC = G * C
    n_groups = seq_len // GC
    lanes = [slice(hd * HEAD, (hd + 1) * HEAD) for hd in range(heads)]

    def pair_masks(n):
        rr = lax.broadcasted_iota(jnp.int32, (n, n), 0)
        cc = lax.broadcasted_iota(jnp.int32, (n, n), 1)
        return cc <= rr, cc >= rr

    mask_c = pair_masks(C)
    mask_h = pair_masks(H)
    dirs = (
        (bf_ref, kf_ref, C - 1, qinf_ref, uf_ref, elf_ref),
        (bb_ref, kb_ref, 0, qinb_ref, ub_ref, elb_ref),
    )

    def split_factors(bq, qv, kv):
        n = bq.shape[0]
        r = 0.5 * (bq[0:1, :] + bq[n - 1:n, :])
        e = jnp.exp(jnp.clip(bq - r, -EXP_CLAMP, EXP_CLAMP))
        return qv * e, kv * (1.0 / e), r, jnp.abs(bq[0:1, :] - r)

    def halves(d, bc, qc, kc, vc):
        first, second = (slice(0, H), slice(H, C)) if d == 0 else (slice(H, C), slice(0, H))
        edge = bc[H - 1:H, :] if d == 0 else bc[H:H + 1, :]
        atts, dev = [], None
        for rows in (first, second):
            qt, kt, _, dev_h = split_factors(bc[rows], qc[rows], kc[rows])
            att = _dot_nt(qt.astype(BF16), kt.astype(BF16))
            atts.append(jnp.where(mask_h[d], att, 0.0).astype(BF16))
            dev = dev_h if dev is None else jnp.maximum(dev, dev_h)
        cross = _dot_nt((qc[second] * jnp.exp(bc[second] - edge)).astype(BF16),
                        (kc[first] * jnp.exp(edge - bc[first])).astype(BF16)).astype(BF16)
        o_first = _dot(atts[0], vc[first])
        o_second = _dot(atts[1], vc[second]) + _dot(cross, vc[first])
        parts = [o_first, o_second] if d == 0 else [o_second, o_first]
        return jnp.concatenate(parts, axis=0), dev

    def pairwise(d, bc, qc, kc, vf):
        s_idx = lax.broadcasted_iota(jnp.int32, (C, 1), 0)

        def row(t, o_blk):
            sel = s_idx == t
            b_t = jnp.sum(jnp.where(sel, bc, 0.0), axis=0, keepdims=True)
            q_t = jnp.sum(jnp.where(sel, qc, 0.0), axis=0, keepdims=True)
            allowed = (s_idx <= t) if d == 0 else (s_idx >= t)
            decay = jnp.where(allowed, jnp.exp(jnp.where(allowed, b_t - bc, 0.0)), 0.0)
            att_t = jnp.sum(decay * (q_t * kc), axis=1, keepdims=True)
            o_t = jnp.sum(att_t * vf, axis=0, keepdims=True)
            return jnp.where(sel, o_t, o_blk)

        return lax.fori_loop(0, C, row, jnp.zeros((C, HEAD), F32))

    def redo_chunk(hd, chunk, rows):
        ls = lanes[hd]
        bcs = [b_ref[rows, ls] for b_ref, _, _, _, _, _ in dirs]
        span = None
        for bc in bcs:
            r = 0.5 * (bc[0:1, :] + bc[C - 1:C, :])
            dev = jnp.maximum(jnp.abs(bc[0:1, :] - r), jnp.abs(r))
            span = dev if span is None else jnp.maximum(span, dev)

        @pl.when(jnp.max(span) > EXP_CLAMP)
        def _():
            qc = q_ref[rows, ls].astype(F32)
            vc = v_ref[rows, ls]
            kcs = [k_ref[rows, ls].astype(F32) for _, k_ref, _, _, _, _ in dirs]
            o_sum, span_h = None, None
            for d, (_, _, last_row, qin_ref, u_ref, _) in enumerate(dirs):
                bc, kc = bcs[d], kcs[d]
                b_last = bc[last_row:last_row + 1, :]
                qin_ref[rows, ls] = (qc * jnp.exp(bc)).astype(BF16)
                u_ref[hd * n_chunks + chunk] = _dot_tn(vc, (kc * jnp.exp(b_last - bc)).astype(BF16))
                o_d, dev = halves(d, bc, qc, kc, vc)
                o_sum = o_d if o_sum is None else o_sum + o_d
                span_h = dev if span_h is None else jnp.maximum(span_h, dev)
            o_ref[rows, ls] = o_sum

            @pl.when(jnp.max(span_h) > EXP_CLAMP)
            def _():
                vf = vc.astype(F32)
                o_ref[rows, ls] = pairwise(0, bcs[0], qc, kcs[0], vf) + pairwise(1, bcs[1], qc, kcs[1], vf)

    def local(j, carry):
        r0 = pl.multiple_of(j * GC, GC)
        work = []
        span = None
        for hd, ls in enumerate(lanes):
            q = q_ref[pl.ds(r0, GC), ls].astype(F32)
            v = v_ref[pl.ds(r0, GC), ls]
            for d, (b_ref, k_ref, last_row, qin_ref, u_ref, el_ref) in enumerate(dirs):
                b = b_ref[pl.ds(r0, GC), ls]
                kk = k_ref[pl.ds(r0, GC), ls].astype(F32)
                for c in range(G):
                    sl = slice(c * C, (c + 1) * C)
                    bc = b[sl]
                    b_last = bc[last_row:last_row + 1, :]
                    qt, kt, r, dev = split_factors(bc, q[sl], kk[sl])
                    dev = jnp.maximum(dev, jnp.abs(r))
                    span = dev if span is None else jnp.maximum(span, dev)
                    qin_ref[pl.ds(r0 + c * C, C), ls] = (qt * jnp.exp(r)).astype(BF16)
                    el_ref[hd * n_chunks + j * G + c] = jnp.exp(b_last)
                    k_out = (kt * jnp.exp(b_last - r)).astype(BF16)
                    work.append((qt.astype(BF16), kt.astype(BF16), k_out, mask_c[d], v[sl], u_ref,
                                 hd * n_chunks + j * G + c))
        atts = [_dot_nt(qt, kt) for qt, kt, _, _, _, _, _ in work]
        atts = [jnp.where(w[3], a, 0.0).astype(BF16) for w, a in zip(work, atts)]
        for (_, _, k_out, _, vc, u_ref, slot) in work:
            u_ref[slot] = _dot_tn(vc, k_out)
        outs = [_dot(a, w[4]) for w, a in zip(work, atts)]
        for hd, ls in enumerate(lanes):
            o_f = jnp.concatenate(outs[(2 * hd) * G:(2 * hd + 1) * G], axis=0)
            o_b = jnp.concatenate(outs[(2 * hd + 1) * G:(2 * hd + 2) * G], axis=0)
            o_ref[pl.ds(r0, GC), ls] = o_f + o_b

        @pl.when(jnp.max(span) > EXP_CLAMP)
        def _():
            for hd in range(heads):
                def redo(c, carry, hd=hd):
                    redo_chunk(hd, j * G + c, pl.ds(pl.multiple_of(j * GC + c * C, C), C))
                    return carry

                lax.fori_loop(0, G, redo, 0)

        return carry

    lax.fori_loop(0, n_groups, local, 0)

    for hd in range(heads):
        base = hd * n_chunks
        if has_init:
            s_init = (init_ref[0, hd].T, init_ref[1, hd].T)
        else:
            s_init = (jnp.zeros((HEAD, HEAD), F32), jnp.zeros((HEAD, HEAD), F32))

        def step(i, carry, base=base):
            s_f, s_b = carry
            kf = base + i
            kb = base + n_chunks - 1 - i
            sf_ref[kf] = s_f.astype(BF16)
            sb_ref[kb] = s_b.astype(BF16)
            return (s_f * elf_ref[kf] + uf_ref[kf], s_b * elb_ref[kb] + ub_ref[kb])

        s_f, s_b = lax.fori_loop(0, n_chunks, step, s_init, unroll=SCAN_UNROLL)
        if emit_state:
            st_ref[0, hd] = s_f.T
            st_ref[1, hd] = s_b.T

    def add_state_readout(j):
        r0 = pl.multiple_of(j * GC, GC)
        for hd, ls in enumerate(lanes):
            o = o_ref[pl.ds(r0, GC), ls]
            for qin_ref, s_ref in ((qinf_ref, sf_ref), (qinb_ref, sb_ref)):
                o = o + jnp.concatenate(
                    [_dot_nt(qin_ref[pl.ds(r0 + c * C, C), ls], s_ref[hd * n_chunks + j * G + c])
                     for c in range(G)], axis=0)
            o_ref[pl.ds(r0, GC), ls] = o

    def normalise(j):
        r0 = pl.multiple_of(j * GC, GC)
        for ls in lanes:
            o = o_ref[pl.ds(r0, GC), ls]
            ms = jnp.mean(o * o, axis=-1, keepdims=True)
            o = o * lax.rsqrt(ms + EPS) * gn_ref[:, ls]
            y_ref[pl.ds(r0, GC), ls] = (o * z_ref[pl.ds(r0, GC), ls].astype(F32)).astype(BF16)

    add_state_readout(0)

    def finish(j, carry):
        normalise(j - 1)
        add_state_readout(j)
        return carry

    lax.fori_loop(1, n_groups, finish, 0)
    normalise(n_groups - 1)


def _scan(pf, pa, gnorm3, init_state, layer, n_seq, seq_len, emit_state, state_acc=None):
    tokens = n_seq * seq_len
    n_chunks = seq_len // SCAN_CHUNK
    has_init = init_state is not None
    has_acc = state_acc is not None
    heads = SCAN_HEADS_SHORT if seq_len <= SCAN_SHORT_SEQ else 1
    width = heads * HEAD
    blk = lambda c0: pl.BlockSpec((seq_len, width), lambda b, h, c0=c0: (b, c0 // heads + h))
    in_specs = [
        blk(0), blk(N_HEADS),
        blk(PA_Q * N_HEADS), blk(PA_V * N_HEADS), blk(PA_Z * N_HEADS),
        blk(PA_KF * N_HEADS), blk((PA_KF + 1) * N_HEADS),
        pl.BlockSpec((None, 1, width), lambda b, h: (layer, 0, h)),
    ]
    args = [pf, pf, pa, pa, pa, pa, pa, gnorm3]
    if has_init:
        in_specs.append(pl.BlockSpec((None, None, 2, heads, HEAD, HEAD),
                                     lambda b, h: (b, layer, 0, h, 0, 0)))
        args.append(init_state)
    aliases = {}
    if has_acc:
        aliases = {len(args): 1}
        in_specs.append(pl.BlockSpec(memory_space=pl.ANY))
        args.append(state_acc)
    out_shape = [jax.ShapeDtypeStruct((tokens, D_A), BF16)]
    out_specs = [pl.BlockSpec((seq_len, width), lambda b, h: (b, h))]
    if emit_state:
        out_shape.append(jax.ShapeDtypeStruct((n_seq, DEPTH, 2, N_HEADS, HEAD, HEAD), F32))
        out_specs.append(pl.BlockSpec((None, None, 2, heads, HEAD, HEAD),
                                      lambda b, h: (b, layer, 0, h, 0, 0)))
    slots = heads * n_chunks
    res = pl.pallas_call(
        functools.partial(_scan_kernel, seq_len, heads, has_init, emit_state, has_acc),
        grid=(n_seq, N_HEADS // heads),
        in_specs=in_specs,
        out_specs=out_specs,
        out_shape=out_shape,
        input_output_aliases=aliases,
        scratch_shapes=[
            pltpu.VMEM((seq_len, width), F32),
            pltpu.VMEM((slots, HEAD, HEAD), BF16), pltpu.VMEM((slots, HEAD, HEAD), BF16),
            pltpu.VMEM((seq_len, width), BF16), pltpu.VMEM((seq_len, width), BF16),
            pltpu.VMEM((slots, HEAD, HEAD), F32), pltpu.VMEM((slots, HEAD, HEAD), F32),
            pltpu.VMEM((slots, 1, HEAD), F32), pltpu.VMEM((slots, 1, HEAD), F32),
        ],
        compiler_params=_cparams(2),
        name="scan",
    )(*args)
    return res if emit_state else (res[0], None)


def _fourier_dense_kernel(scale, zr_ref, zi_ref, c_ref, s_ref, y_ref):
    yf = (_dot(c_ref[...].astype(BF16), zr_ref[...].astype(BF16))
          + _dot(s_ref[...].astype(BF16), zi_ref[...].astype(BF16)))
    y_ref[...] = (yf * scale).astype(BF16)


def _fourier_dense(zr, zi, n_seq, seq_len):
    c, s = _dft_cos_sin(seq_len)
    scale = 1.0 / np.sqrt(seq_len * GROUP)
    blk = pl.BlockSpec((seq_len, D_B), lambda b: (b, 0))
    mat = pl.BlockSpec((seq_len, seq_len), lambda b: (0, 0))
    return pl.pallas_call(
        functools.partial(_fourier_dense_kernel, scale),
        grid=(n_seq,),
        in_specs=[blk, blk, mat, mat],
        out_specs=blk,
        out_shape=jax.ShapeDtypeStruct((n_seq * seq_len, D_B), BF16),
        compiler_params=_cparams(1),
        name="fourier_dense",
    )(zr, zi, jnp.asarray(c, F32), jnp.asarray(s, F32))


def _fourier_stage1_kernel(zr_ref, zi_ref, m_ref, tc_ref, ts_ref, br_ref, bi_ref):
    m = m_ref[...].astype(BF16)
    for n in range(FFT_BLOCK):
        z = jnp.concatenate([zr_ref[:, n, :], zi_ref[:, n, :]], axis=0).astype(BF16)
        a = _dot(m, z)
        ar = a[:FFT_RADIX]
        ai = a[FFT_RADIX:]
        tc = jnp.concatenate([tc_ref[n]] * N_GROUPS, axis=1)
        ts = jnp.concatenate([ts_ref[n]] * N_GROUPS, axis=1)
        br_ref[:, n, :] = ar * tc + ai * ts
        bi_ref[:, n, :] = ai * tc - ar * ts


def _fourier_stage2_kernel(scale, br_ref, bi_ref, cs_ref, y_ref):
    cs = cs_ref[...].astype(BF16)
    for j in range(FFT_BLOCK):
        bcat = jnp.concatenate([br_ref[j], bi_ref[j]], axis=0).astype(BF16)
        y_ref[:, j, :] = _dot(cs, bcat) * scale


def _fourier_two_stage(zr, zi, n_seq, seq_len):
    R = FFT_RADIX
    assert seq_len == R * R
    c, s = _dft_cos_sin(R)
    m1 = np.block([[c, s], [-s, c]])
    idx = np.arange(R, dtype=np.float64)
    ang = 2.0 * np.pi * np.outer(idx, idx) / seq_len
    tc = np.repeat(np.cos(ang)[:, :, None], GROUP, axis=2).astype(np.float32)
    ts = np.repeat(np.sin(ang)[:, :, None], GROUP, axis=2).astype(np.float32)
    cs = np.concatenate([c, s], axis=1)
    scale = 1.0 / np.sqrt(seq_len * GROUP)

    fast_blk = pl.BlockSpec((None, R, FFT_BLOCK, D_B), lambda b, j: (b, 0, j, 0))
    slow_blk = pl.BlockSpec((None, FFT_BLOCK, R, D_B), lambda b, j: (b, j, 0, 0))
    tw_blk = pl.BlockSpec((FFT_BLOCK, R, GROUP), lambda b, j: (j, 0, 0))
    shape4 = jax.ShapeDtypeStruct((n_seq, R, R, D_B), F32)
    br, bi = pl.pallas_call(
        _fourier_stage1_kernel,
        grid=(n_seq, R // FFT_BLOCK),
        in_specs=[fast_blk, fast_blk, pl.BlockSpec((2 * R, 2 * R), lambda b, j: (0, 0)), tw_blk, tw_blk],
        out_specs=(fast_blk, fast_blk),
        out_shape=(shape4, shape4),
        compiler_params=_cparams(2),
        name="fourier_stage1",
    )(zr.reshape(n_seq, R, R, D_B), zi.reshape(n_seq, R, R, D_B),
      jnp.asarray(m1, F32), jnp.asarray(tc), jnp.asarray(ts))

    y = pl.pallas_call(
        functools.partial(_fourier_stage2_kernel, scale),
        grid=(n_seq, R // FFT_BLOCK),
        in_specs=[slow_blk, slow_blk, pl.BlockSpec((R, 2 * R), lambda b, j: (0, 0))],
        out_specs=fast_blk,
        out_shape=shape4,
        compiler_params=_cparams(2),
        name="fourier_stage2",
    )(br, bi, jnp.asarray(cs, F32))
    return y.reshape(n_seq * seq_len, D_B)


def _merge_kernel(has_pe, final, *refs):
    refs = list(refs)
    ya_ref, yf_ref, zb_ref, ga_ref, gb_ref, x_ref = refs[:6]
    pos = 6
    x = x_ref[...]
    if has_pe:
        x = x + refs[pos][...]
        pos += 1
    mod_ref, wpa_ref, wpb_ref, wo_ref = refs[pos:pos + 4]
    pos += 4
    if final:
        fw_ref = refs[pos]
        pos += 1
    out_ref = refs[pos]
    yb = (yf_ref[...].astype(F32) * zb_ref[...].astype(F32)).astype(BF16)
    merged = (ga_ref[...].astype(F32) * _dot(ya_ref[...], wpa_ref[...])
              + gb_ref[...].astype(F32) * _dot(yb, wpb_ref[...]))
    out = _dot(merged.astype(BF16), wo_ref[...])
    xn = x + mod_ref[:, 2 * D:3 * D] * out
    if final:
        ms = jnp.mean(xn * xn, axis=-1, keepdims=True)
        xn = xn * lax.rsqrt(ms + EPS) * fw_ref[...]
    out_ref[...] = xn


def _merge(ya, yf, zb, pa, x, pe, mod, wpa, wpb, wo, final_w, layer, seq_len, latent):
    tokens = x.shape[0]
    tm = TM_MERGE
    has_pe = pe is not None
    final = final_w is not None
    row = lambda i: (i, 0)
    if latent:
        mod_idx = lambda i: (layer * MOD_ROWS + 1 + (i * tm) // seq_len, 0, 0)
    else:
        mod_idx = lambda i: (layer * MOD_ROWS, 0, 0)
    in_specs = [
        pl.BlockSpec((tm, D_A), row),
        pl.BlockSpec((tm, D_B), row),
        pl.BlockSpec((tm, D_B), row),
        pl.BlockSpec((tm, D), lambda i: (i, PA_GA)),
        pl.BlockSpec((tm, D), lambda i: (i, PA_GB)),
        pl.BlockSpec((tm, D), row),
    ]
    args = [ya, yf, zb, pa, pa, x]
    if has_pe:
        per_seq = seq_len // tm
        in_specs.append(pl.BlockSpec((tm, D), lambda i: (i % per_seq, 0)))
        args.append(pe)
    in_specs += [
        pl.BlockSpec((None, 1, 3 * D), mod_idx),
        pl.BlockSpec((None, D_A, D), lambda i: (layer, 0, 0)),
        pl.BlockSpec((None, D_B, D), lambda i: (layer, 0, 0)),
        pl.BlockSpec((None, D, D), lambda i: (layer, 0, 0)),
    ]
    args += [mod, wpa, wpb, wo]
    if final:
        in_specs.append(pl.BlockSpec((1, D), lambda i: (0, 0)))
        args.append(final_w)
    return pl.pallas_call(
        functools.partial(_merge_kernel, has_pe, final),
        grid=(tokens // tm,),
        in_specs=in_specs,
        out_specs=pl.BlockSpec((tm, D), row),
        out_shape=jax.ShapeDtypeStruct((tokens, D), F32),
        compiler_params=_cparams(1),
        name="merge",
    )(*args)


def kernel(x_prompt, x_sample, state_hgrn, c, c_ctx, norm_w, w_ada, b_ada, w_in, lb_raw,
           gnorm_w, w_pa, w_pb, w_o, final_norm_w):
    n_p, len_p, _ = x_prompt.shape
    n_s, len_s, _ = x_sample.shape

    cond = jnp.zeros((MOD_ROWS, D), F32).at[0].set(c_ctx).at[1:1 + n_s].set(c)
    mod = _modulation(cond, w_ada, b_ada)
    lbs = _lower_bounds(lb_raw)

    w_in_bf = w_in.astype(BF16)
    wpa_bf = w_pa.astype(BF16)
    wpb_bf = w_pb.astype(BF16)
    wo_bf = w_o.astype(BF16)
    norm_w3 = norm_w.reshape(DEPTH, 1, D)
    gnorm3 = gnorm_w.reshape(DEPTH, 1, D_A)
    final_w = final_norm_w.reshape(1, D)

    cc_np, sc_np = _dft_cos_sin(GROUP)
    cc = jnp.asarray(cc_np, F32)
    sc = jnp.asarray(sc_np, F32)
    pe = jnp.asarray(_pos_embed_table(len_s, D))

    xp = x_prompt.reshape(n_p * len_p, D)
    xs = x_sample.reshape(n_s * len_s, D)
    new_state = None
    for l in range(DEPTH):
        last = l == DEPTH - 1
        fw = final_w if last else None
        pe_l = pe if l == 0 else None

        pf, pa, zr, zi, zb = _proj(xp, None, mod, norm_w3, lbs, w_in_bf, cc, sc, l, len_p, False)
        ya, new_state = _scan(pf, pa, gnorm3, None, l, n_p, len_p, True, new_state)
        yf = _fourier_dense(zr, zi, n_p, len_p)
        xp = _merge(ya, yf, zb, pa, xp, None, mod, wpa_bf, wpb_bf, wo_bf, fw, l, len_p, False)

        pf, pa, zr, zi, zb = _proj(xs, pe_l, mod, norm_w3, lbs, w_in_bf, cc, sc, l, len_s, True)
        ya, _ = _scan(pf, pa, gnorm3, state_hgrn, l, n_s, len_s, False)
        yf = _fourier_two_stage(zr, zi, n_s, len_s)
        xs = _merge(ya, yf, zb, pa, xs, pe_l, mod, wpa_bf, wpb_bf, wo_bf, fw, l, len_s, True)

    y_prompt = xp.reshape(n_p, len_p, D)
    y_sample = xs.reshape(n_s, len_s, D)
    return (y_prompt, y_sample, new_state)
```

```python
import functools

import jax
import jax.numpy as jnp
import numpy as np
from jax import lax
from jax.experimental import pallas as pl
from jax.experimental.pallas import tpu as pltpu

D = 1024
DEPTH = 4
N_HEADS = 8
HEAD = 128
D_A = N_HEADS * HEAD
N_GROUPS = 4
GROUP = 128
D_B = N_GROUPS * GROUP
D_IN = 5 * D_A + 2 * D_B + 2 * D
GRID_W = 64
EPS = 1e-6

V7X_VMEM_LIMIT_BYTES = 56 * 1024 * 1024

SCAN_CHUNK = 64
SCAN_GROUP = 8
SCAN_UNROLL = 4
SCAN_SHORT_SEQ = 512
SCAN_HEADS_SHORT = 4
EXP_CLAMP = 80.0
TM_PROJ = 256
CUM_ROWS = 256
TM_MERGE = 512
FFT_RADIX = 64
FFT_BLOCK = 32
MOD_ROWS = 8

F32 = jnp.float32
BF16 = jnp.bfloat16


def _cparams(n_axes):
    return pltpu.CompilerParams(
        dimension_semantics=("arbitrary",) * n_axes,
        vmem_limit_bytes=V7X_VMEM_LIMIT_BYTES,
    )


def _dot(a, b):
    return jnp.dot(a, b, preferred_element_type=F32)


def _dot_nt(a, b):
    return lax.dot_general(a, b, (((1,), (1,)), ((), ())), preferred_element_type=F32)


def _dot_tn(a, b):
    return lax.dot_general(a, b, (((0,), (0,)), ((), ())), preferred_element_type=F32)


def _sigmoid(x):
    return 1.0 / (1.0 + jnp.exp(-x))


def _silu(x):
    return x * _sigmoid(x)


def _dft_cos_sin(n):
    k = np.arange(n, dtype=np.float64)
    ang = 2.0 * np.pi * np.outer(k, k) / n
    return np.cos(ang), np.sin(ang)


def _pos_embed_table(length, d):
    rows = length // GRID_W
    r = np.repeat(np.arange(rows, dtype=np.float64), GRID_W)
    col = np.tile(np.arange(GRID_W, dtype=np.float64), rows)
    nf = d // 4
    freqs = 1.0 / (10000.0 ** (np.arange(nf, dtype=np.float64) / nf))

    def emb(p):
        a = p[:, None] * freqs[None, :]
        return np.concatenate([np.sin(a), np.cos(a)], axis=-1)

    return np.concatenate([emb(r), emb(col)], axis=-1).astype(np.float32)


def _lb_kernel(raw_ref, out_ref):
    x = raw_ref[...]
    m = jnp.max(x, axis=0, keepdims=True)
    e = jnp.exp(x - m)
    p = e / jnp.sum(e, axis=0, keepdims=True)
    cs = p[0:1]
    first = cs
    out_ref[0:1, :] = cs - first
    for l in range(1, DEPTH):
        cs = cs + p[l:l + 1]
        out_ref[l:l + 1, :] = cs - first


def _lower_bounds(lb_raw):
    raw = lb_raw.reshape(DEPTH, 2 * D_A)
    out = pl.pallas_call(
        _lb_kernel,
        out_shape=jax.ShapeDtypeStruct((DEPTH, 2 * D_A), F32),
        name="lower_bounds",
    )(raw)
    return out.reshape(DEPTH * 2, 1, D_A)


def _mod_kernel(cond_ref, w_ref, b_ref, out_ref):
    a = _silu(cond_ref[...])
    out_ref[...] = jnp.dot(a, w_ref[...], preferred_element_type=F32,
                           precision=lax.Precision.HIGHEST) + b_ref[...]


def _modulation(cond, w_ada, b_ada):
    tn = 512
    out = pl.pallas_call(
        _mod_kernel,
        grid=(DEPTH, 3 * D // tn),
        in_specs=[
            pl.BlockSpec((MOD_ROWS, D), lambda l, j: (0, 0)),
            pl.BlockSpec((None, D, tn), lambda l, j: (l, 0, j)),
            pl.BlockSpec((None, 1, tn), lambda l, j: (l, 0, j)),
        ],
        out_specs=pl.BlockSpec((None, MOD_ROWS, tn), lambda l, j: (l, 0, j)),
        out_shape=jax.ShapeDtypeStruct((DEPTH, MOD_ROWS, 3 * D), F32),
        compiler_params=_cparams(2),
        name="modulation",
    )(cond, w_ada, b_ada.reshape(DEPTH, 1, 3 * D))
    return out.reshape(DEPTH * MOD_ROWS, 1, 3 * D)


_Q0, _FF0, _V0, _ZA0, _U0, _ZB0, _GA0, _GB0 = (
    0, D_A, 3 * D_A, 4 * D_A, 5 * D_A, 5 * D_A + D_B, 5 * D_A + 2 * D_B, 5 * D_A + 2 * D_B + D)
PA_Q, PA_V, PA_Z, PA_GA, PA_GB, PA_KF = 0, 1, 2, 3, 4, 5
PA_WIDTH = 7 * D


def _forget_gate(x, lb):
    f = lb + (1.0 - lb) * _sigmoid(x)
    log_f = jnp.where(f < 1e-30, jnp.minimum(x, 0.0), jnp.log(jnp.maximum(f, 1e-30)))
    return log_f, 1.0 - f


def _proj_kernel(has_pe, *refs):
    refs = list(refs)
    x = refs[0][...]
    pos = 1
    if has_pe:
        x = x + refs[pos][...]
        pos += 1
    mod_ref, nw_ref, lbf_ref, lbb_ref, w_ref, cc_ref, sc_ref = refs[pos:pos + 7]
    pf_ref, pa_ref, zr_ref, zi_ref, zb_ref = refs[pos + 7:]

    ms = jnp.mean(x * x, axis=-1, keepdims=True)
    y = x * lax.rsqrt(ms + EPS) * nw_ref[...]
    shift = mod_ref[:, 0:D]
    scale = mod_ref[:, D:2 * D]
    h = (y * (1.0 + scale) + shift).astype(BF16)

    def sec(c0, width):
        return _dot(h, w_ref[:, c0:c0 + width])

    def put(block, val):
        pa_ref[:, block * D:(block + 1) * D] = val.astype(BF16)

    row = lax.broadcasted_iota(jnp.int32, (CUM_ROWS, CUM_ROWS), 0)
    col = lax.broadcasted_iota(jnp.int32, (CUM_ROWS, CUM_ROWS), 1)
    same_chunk = (row // SCAN_CHUNK) == (col // SCAN_CHUNK)
    tris = (jnp.where(same_chunk & (col <= row), 1.0, 0.0).astype(BF16),
            jnp.where(same_chunk & (col >= row), 1.0, 0.0).astype(BF16))
    for d, lb_ref in enumerate((lbf_ref, lbb_ref)):
        g, kk = _forget_gate(sec(_FF0 + d * D_A, D_A), lb_ref[...])
        g_hi = g.astype(BF16)
        g_lo = (g - g_hi.astype(F32)).astype(BF16)
        for r0 in range(0, x.shape[0], CUM_ROWS):
            rows = slice(r0, r0 + CUM_ROWS)
            pf_ref[rows, d * D_A:(d + 1) * D_A] = _dot(tris[d], g_hi[rows]) + _dot(tris[d], g_lo[rows])
        put(PA_KF + d, kk)

    put(PA_Q, _silu(sec(_Q0, D_A)))
    put(PA_V, sec(_V0, D_A))
    put(PA_Z, _silu(sec(_ZA0, D_A)))
    put(PA_GA, _sigmoid(sec(_GA0, D)))
    put(PA_GB, _sigmoid(sec(_GB0, D)))
    zb_ref[...] = _silu(sec(_ZB0, D_B)).astype(BF16)
    u = sec(_U0, D_B).astype(BF16)
    cc = cc_ref[...].astype(BF16)
    sc = sc_ref[...].astype(BF16)
    for g in range(N_GROUPS):
        ug = u[:, g * GROUP:(g + 1) * GROUP]
        zr_ref[:, g * GROUP:(g + 1) * GROUP] = _dot(ug, cc)
        zi_ref[:, g * GROUP:(g + 1) * GROUP] = -_dot(ug, sc)


def _proj(x, pe, mod, norm_w3, lbs, w_in_bf, cc, sc, layer, seq_len, latent):
    tokens = x.shape[0]
    tm = TM_PROJ
    has_pe = pe is not None
    per_seq = seq_len // tm
    if latent:
        mod_idx = lambda i: (layer * MOD_ROWS + 1 + i // per_seq, 0, 0)
    else:
        mod_idx = lambda i: (layer * MOD_ROWS, 0, 0)
    in_specs = [pl.BlockSpec((tm, D), lambda i: (i, 0))]
    args = [x]
    if has_pe:
        in_specs.append(pl.BlockSpec((tm, D), lambda i: (i % per_seq, 0)))
        args.append(pe)
    in_specs += [
        pl.BlockSpec((None, 1, 3 * D), mod_idx),
        pl.BlockSpec((None, 1, D), lambda i: (layer, 0, 0)),
        pl.BlockSpec((None, 1, D_A), lambda i: (2 * layer, 0, 0)),
        pl.BlockSpec((None, 1, D_A), lambda i: (2 * layer + 1, 0, 0)),
        pl.BlockSpec((None, D, D_IN), lambda i: (layer, 0, 0), pipeline_mode=pl.Buffered(1)),
        pl.BlockSpec((GROUP, GROUP), lambda i: (0, 0)),
        pl.BlockSpec((GROUP, GROUP), lambda i: (0, 0)),
    ]
    args += [mod, norm_w3, lbs, lbs, w_in_bf, cc, sc]
    out_shape = (
        jax.ShapeDtypeStruct((tokens, 2 * D_A), F32),
        jax.ShapeDtypeStruct((tokens, PA_WIDTH), BF16),
        jax.ShapeDtypeStruct((tokens, D_B), F32),
        jax.ShapeDtypeStruct((tokens, D_B), F32),
        jax.ShapeDtypeStruct((tokens, D_B), BF16),
    )
    out_specs = (
        pl.BlockSpec((tm, 2 * D_A), lambda i: (i, 0)),
        pl.BlockSpec((tm, PA_WIDTH), lambda i: (i, 0)),
        pl.BlockSpec((tm, D_B), lambda i: (i, 0)),
        pl.BlockSpec((tm, D_B), lambda i: (i, 0)),
        pl.BlockSpec((tm, D_B), lambda i: (i, 0)),
    )
    return pl.pallas_call(
        functools.partial(_proj_kernel, has_pe),
        grid=(tokens // tm,),
        in_specs=in_specs,
        out_specs=out_specs,
        out_shape=out_shape,
        compiler_params=_cparams(1),
        name="proj",
    )(*args)


def _scan_kernel(seq_len, heads, has_init, emit_state, has_acc, *refs):
    refs = list(refs)
    bf_ref, bb_ref, q_ref, v_ref, z_ref, kf_ref, kb_ref, gn_ref = refs[:8]
    pos = 8
    init_ref = None
    if has_init:
        init_ref = refs[pos]
        pos += 1
    if has_acc:
        pos += 1
    y_ref = refs[pos]
    pos += 1
    st_ref = None
    if emit_state:
        st_ref = refs[pos]
        pos += 1
    (o_ref, sf_ref, sb_ref, qinf_ref, qinb_ref, uf_ref, ub_ref, elf_ref, elb_ref) = refs[pos:pos + 9]

    C = SCAN_CHUNK
    H = C // 2
    n_chunks = seq_len // C
    G = min(SCAN_GROUP, n_chunks)
    GC = G * C
    n_groups = seq_len // GC
    lanes = [slice(hd * HEAD, (hd + 1) * HEAD) for hd in range(heads)]

    def pair_masks(n):
        rr = lax.broadcasted_iota(jnp.int32, (n, n), 0)
        cc = lax.broadcasted_iota(jnp.int32, (n, n), 1)
        return cc <= rr, cc >= rr

    mask_c = pair_masks(C)
    mask_h = pair_masks(H)
    dirs = (
        (bf_ref, kf_ref, C - 1, qinf_ref, uf_ref, elf_ref),
        (bb_ref, kb_ref, 0, qinb_ref, ub_ref, elb_ref),
    )

    def split_factors(bq, qv, kv):
        n = bq.shape[0]
        r = 0.5 * (bq[0:1, :] + bq[n - 1:n, :])
        e = jnp.exp(jnp.clip(bq - r, -EXP_CLAMP, EXP_CLAMP))
        return qv * e, kv * (1.0 / e), r, jnp.abs(bq[0:1, :] - r)

    def halves(d, bc, qc, kc, vc):
        first, second = (slice(0, H), slice(H, C)) if d == 0 else (slice(H, C), slice(0, H))
        edge = bc[H - 1:H, :] if d == 0 else bc[H:H + 1, :]
        atts, dev = [], None
        for rows in (first, second):
            qt, kt, _, dev_h = split_factors(bc[rows], qc[rows], kc[rows])
            att = _dot_nt(qt.astype(BF16), kt.astype(BF16))
            atts.append(jnp.where(mask_h[d], att, 0.0).astype(BF16))
            dev = dev_h if dev is None else jnp.maximum(dev, dev_h)
        cross = _dot_nt((qc[second] * jnp.exp(bc[second] - edge)).astype(BF16),
                        (kc[first] * jnp.exp(edge - bc[first])).astype(BF16)).astype(BF16)
        o_first = _dot(atts[0], vc[first])
        o_second = _dot(atts[1], vc[second]) + _dot(cross, vc[first])
        parts = [o_first, o_second] if d == 0 else [o_second, o_first]
        return jnp.concatenate(parts, axis=0), dev

    def pairwise(d, bc, qc, kc, vf):
        s_idx = lax.broadcasted_iota(jnp.int32, (C, 1), 0)

        def row(t, o_blk):
            sel = s_idx == t
            b_t = jnp.sum(jnp.where(sel, bc, 0.0), axis=0, keepdims=True)
            q_t = jnp.sum(jnp.where(sel, qc, 0.0), axis=0, keepdims=True)
            allowed = (s_idx <= t) if d == 0 else (s_idx >= t)
            decay = jnp.where(allowed, jnp.exp(jnp.where(allowed, b_t - bc, 0.0)), 0.0)
            att_t = jnp.sum(decay * (q_t * kc), axis=1, keepdims=True)
            o_t = jnp.sum(att_t * vf, axis=0, keepdims=True)
            return jnp.where(sel, o_t, o_blk)

        return lax.fori_loop(0, C, row, jnp.zeros((C, HEAD), F32))

    def redo_chunk(hd, chunk, rows):
        ls = lanes[hd]
        bcs = [b_ref[rows, ls] for b_ref, _, _, _, _, _ in dirs]
        span = None
        for bc in bcs:
            r = 0.5 * (bc[0:1, :] + bc[C - 1:C, :])
            dev = jnp.maximum(jnp.abs(bc[0:1, :] - r), jnp.abs(r))
            span = dev if span is None else jnp.maximum(span, dev)

        @pl.when(jnp.max(span) > EXP_CLAMP)
        def _():
            qc = q_ref[rows, ls].astype(F32)
            vc = v_ref[rows, ls]
            kcs = [k_ref[rows, ls].astype(F32) for _, k_ref, _, _, _, _ in dirs]
            o_sum, span_h = None, None
            for d, (_, _, last_row, qin_ref, u_ref, _) in enumerate(dirs):
                bc, kc = bcs[d], kcs[d]
                b_last = bc[last_row:last_row + 1, :]
                qin_ref[rows, ls] = (qc * jnp.exp(bc)).astype(BF16)
                u_ref[hd * n_chunks + chunk] = _dot_tn(vc, (kc * jnp.exp(b_last - bc)).astype(BF16))
                o_d, dev = halves(d, bc, qc, kc, vc)
                o_sum = o_d if o_sum is None else o_sum + o_d
                span_h = dev if span_h is None else jnp.maximum(span_h, dev)
            o_ref[rows, ls] = o_sum

            @pl.when(jnp.max(span_h) > EXP_CLAMP)
            def _():
                vf = vc.astype(F32)
                o_ref[rows, ls] = pairwise(0, bcs[0], qc, kcs[0], vf) + pairwise(1, bcs[1], qc, kcs[1], vf)

    def local(j, carry):
        r0 = pl.multiple_of(j * GC, GC)
        work = []
        span = None
        for hd, ls in enumerate(lanes):
            q = q_ref[pl.ds(r0, GC), ls].astype(F32)
            v = v_ref[pl.ds(r0, GC), ls]
            for d, (b_ref, k_ref, last_row, qin_ref, u_ref, el_ref) in enumerate(dirs):
                b = b_ref[pl.ds(r0, GC), ls]
                kk = k_ref[pl.ds(r0, GC), ls].astype(F32)
                for c in range(G):
                    sl = slice(c * C, (c + 1) * C)
                    bc = b[sl]
                    b_last = bc[last_row:last_row + 1, :]
                    qt, kt, r, dev = split_factors(bc, q[sl], kk[sl])
                    dev = jnp.maximum(dev, jnp.abs(r))
                    span = dev if span is None else jnp.maximum(span, dev)
                    qin_ref[pl.ds(r0 + c * C, C), ls] = (qt * jnp.exp(r)).astype(BF16)
                    el_ref[hd * n_chunks + j * G + c] = jnp.exp(b_last)
                    k_out = (kt * jnp.exp(b_last - r)).astype(BF16)
                    work.append((qt.astype(BF16), kt.astype(BF16), k_out, mask_c[d], v[sl], u_ref,
                                 hd * n_chunks + j * G + c))
        atts = [_dot_nt(qt, kt) for qt, kt, _, _, _, _, _ in work]
        atts = [jnp.where(w[3], a, 0.0).astype(BF16) for w, a in zip(work, atts)]
        for (_, _, k_out, _, vc, u_ref, slot) in work:
            u_ref[slot] = _dot_tn(vc, k_out)
        outs = [_dot(a, w[4]) for w, a in zip(work, atts)]
        for hd, ls in enumerate(lanes):
            o_f = jnp.concatenate(outs[(2 * hd) * G:(2 * hd + 1) * G], axis=0)
            o_b = jnp.concatenate(outs[(2 * hd + 1) * G:(2 * hd + 2) * G], axis=0)
            o_ref[pl.ds(r0, GC), ls] = o_f + o_b

        @pl.when(jnp.max(span) > EXP_CLAMP)
        def _():
            for hd in range(heads):
                def redo(c, carry, hd=hd):
                    redo_chunk(hd, j * G + c, pl.ds(pl.multiple_of(j * GC + c * C, C), C))
                    return carry

                lax.fori_loop(0, G, redo, 0)

        return carry

    lax.fori_loop(0, n_groups, local, 0)

    for hd in range(heads):
        base = hd * n_chunks
        if has_init:
            s_init = (init_ref[0, hd].T, init_ref[1, hd].T)
        else:
            s_init = (jnp.zeros((HEAD, HEAD), F32), jnp.zeros((HEAD, HEAD), F32))

        def step(i, carry, base=base):
            s_f, s_b = carry
            kf = base + i
            kb = base + n_chunks - 1 - i
            sf_ref[kf] = s_f.astype(BF16)
            sb_ref[kb] = s_b.astype(BF16)
            return (s_f * elf_ref[kf] + uf_ref[kf], s_b * elb_ref[kb] + ub_ref[kb])

        s_f, s_b = lax.fori_loop(0, n_chunks, step, s_init, unroll=SCAN_UNROLL)
        if emit_state:
            st_ref[0, hd] = s_f.T
            st_ref[1, hd] = s_b.T

    def add_state_readout(j):
        r0 = pl.multiple_of(j * GC, GC)
        for hd, ls in enumerate(lanes):
            o = o_ref[pl.ds(r0, GC), ls]
            for qin_ref, s_ref in ((qinf_ref, sf_ref), (qinb_ref, sb_ref)):
                o = o + jnp.concatenate(
                    [_dot_nt(qin_ref[pl.ds(r0 + c * C, C), ls], s_ref[hd * n_chunks + j * G + c])
                     for c in range(G)], axis=0)
            o_ref[pl.ds(r0, GC), ls] = o

    def normalise(j):
        r0 = pl.multiple_of(j * GC, GC)
        for ls in lanes:
            o = o_ref[pl.ds(r0, GC), ls]
            ms = jnp.mean(o * o, axis=-1, keepdims=True)
            o = o * lax.rsqrt(ms + EPS) * gn_ref[:, ls]
            y_ref[pl.ds(r0, GC), ls] = (o * z_ref[pl.ds(r0, GC), ls].astype(F32)).astype(BF16)

    add_state_readout(0)

    def finish(j, carry):
        normalise(j - 1)
        add_state_readout(j)
        return carry

    lax.fori_loop(1, n_groups, finish, 0)
    normalise(n_groups - 1)


def _scan(pf, pa, gnorm3, init_state, layer, n_seq, seq_len, emit_state, state_acc=None):
    tokens = n_seq * seq_len
    n_chunks = seq_len // SCAN_CHUNK
    has_init = init_state is not None
    has_acc = state_acc is not None
    heads = SCAN_HEADS_SHORT if seq_len <= SCAN_SHORT_SEQ else 1
    width = heads * HEAD
    blk = lambda c0: pl.BlockSpec((seq_len, width), lambda b, h, c0=c0: (b, c0 // heads + h))
    in_specs = [
        blk(0), blk(N_HEADS),
        blk(PA_Q * N_HEADS), blk(PA_V * N_HEADS), blk(PA_Z * N_HEADS),
        blk(PA_KF * N_HEADS), blk((PA_KF + 1) * N_HEADS),
        pl.BlockSpec((None, 1, width), lambda b, h: (layer, 0, h)),
    ]
    args = [pf, pf, pa, pa, pa, pa, pa, gnorm3]
    if has_init:
        in_specs.append(pl.BlockSpec((None, None, 2, heads, HEAD, HEAD),
                                     lambda b, h: (b, layer, 0, h, 0, 0)))
        args.append(init_state)
    aliases = {}
    if has_acc:
        aliases = {len(args): 1}
        in_specs.append(pl.BlockSpec(memory_space=pl.ANY))
        args.append(state_acc)
    out_shape = [jax.ShapeDtypeStruct((tokens, D_A), BF16)]
    out_specs = [pl.BlockSpec((seq_len, width), lambda b, h: (b, h))]
    if emit_state:
        out_shape.append(jax.ShapeDtypeStruct((n_seq, DEPTH, 2, N_HEADS, HEAD, HEAD), F32))
        out_specs.append(pl.BlockSpec((None, None, 2, heads, HEAD, HEAD),
                                      lambda b, h: (b, layer, 0, h, 0, 0)))
    slots = heads * n_chunks
    res = pl.pallas_call(
        functools.partial(_scan_kernel, seq_len, heads, has_init, emit_state, has_acc),
        grid=(n_seq, N_HEADS // heads),
        in_specs=in_specs,
        out_specs=out_specs,
        out_shape=out_shape,
        input_output_aliases=aliases,
        scratch_shapes=[
            pltpu.VMEM((seq_len, width), F32),
            pltpu.VMEM((slots, HEAD, HEAD), BF16), pltpu.VMEM((slots, HEAD, HEAD), BF16),
            pltpu.VMEM((seq_len, width), BF16), pltpu.VMEM((seq_len, width), BF16),
            pltpu.VMEM((slots, HEAD, HEAD), F32), pltpu.VMEM((slots, HEAD, HEAD), F32),
            pltpu.VMEM((slots, 1, HEAD), F32), pltpu.VMEM((slots, 1, HEAD), F32),
        ],
        compiler_params=_cparams(2),
        name="scan",
    )(*args)
    return res if emit_state else (res[0], None)


def _fourier_stage1_kernel(zr_ref, zi_ref, m_ref, tc_ref, ts_ref, br_ref, bi_ref):
    m = m_ref[...].astype(BF16)
    for n in range(FFT_BLOCK):
        z = jnp.concatenate([zr_ref[:, n, :], zi_ref[:, n, :]], axis=0).astype(BF16)
        a = _dot(m, z)
        ar = a[:FFT_RADIX]
        ai = a[FFT_RADIX:]
        tc = jnp.concatenate([tc_ref[n]] * N_GROUPS, axis=1)
        ts = jnp.concatenate([ts_ref[n]] * N_GROUPS, axis=1)
        br_ref[:, n, :] = ar * tc + ai * ts
        bi_ref[:, n, :] = ai * tc - ar * ts


def _fourier_stage2_kernel(scale, br_ref, bi_ref, cs_ref, y_ref):
    cs = cs_ref[...].astype(BF16)
    for j in range(FFT_BLOCK):
        bcat = jnp.concatenate([br_ref[j], bi_ref[j]], axis=0).astype(BF16)
        y_ref[:, j, :] = _dot(cs, bcat) * scale


def _fourier_two_stage(zr, zi, n_seq, seq_len):
    R = FFT_RADIX
    assert seq_len == R * R
    c, s = _dft_cos_sin(R)
    m1 = np.block([[c, s], [-s, c]])
    idx = np.arange(R, dtype=np.float64)
    ang = 2.0 * np.pi * np.outer(idx, idx) / seq_len
    tc = np.repeat(np.cos(ang)[:, :, None], GROUP, axis=2).astype(np.float32)
    ts = np.repeat(np.sin(ang)[:, :, None], GROUP, axis=2).astype(np.float32)
    cs = np.concatenate([c, s], axis=1)
    scale = 1.0 / np.sqrt(seq_len * GROUP)

    fast_blk = pl.BlockSpec((None, R, FFT_BLOCK, D_B), lambda b, j: (b, 0, j, 0))
    slow_blk = pl.BlockSpec((None, FFT_BLOCK, R, D_B), lambda b, j: (b, j, 0, 0))
    tw_blk = pl.BlockSpec((FFT_BLOCK, R, GROUP), lambda b, j: (j, 0, 0))
    shape4 = jax.ShapeDtypeStruct((n_seq, R, R, D_B), F32)
    br, bi = pl.pallas_call(
        _fourier_stage1_kernel,
        grid=(n_seq, R // FFT_BLOCK),
        in_specs=[fast_blk, fast_blk, pl.BlockSpec((2 * R, 2 * R), lambda b, j: (0, 0)), tw_blk, tw_blk],
        out_specs=(fast_blk, fast_blk),
        out_shape=(shape4, shape4),
        compiler_params=_cparams(2),
        name="fourier_stage1",
    )(zr.reshape(n_seq, R, R, D_B), zi.reshape(n_seq, R, R, D_B),
      jnp.asarray(m1, F32), jnp.asarray(tc), jnp.asarray(ts))

    y = pl.pallas_call(
        functools.partial(_fourier_stage2_kernel, scale),
        grid=(n_seq, R // FFT_BLOCK),
        in_specs=[slow_blk, slow_blk, pl.BlockSpec((R, 2 * R), lambda b, j: (0, 0))],
        out_specs=fast_blk,
        out_shape=shape4,
        compiler_params=_cparams(2),
        name="fourier_stage2",
    )(br, bi, jnp.asarray(cs, F32))
    return y.reshape(n_seq * seq_len, D_B)


def _merge_kernel(has_pe, final, dense_len, *refs):
    refs = list(refs)
    if dense_len:
        ya_ref, zr_ref, zi_ref, c_ref, s_ref, zb_ref, ga_ref, gb_ref, x_ref = refs[:9]
        pos = 9
        c = c_ref[...].astype(BF16)
        s = s_ref[...].astype(BF16)
        parts = []
        for r0 in range(0, x_ref.shape[0], dense_len):
            rows = slice(r0, r0 + dense_len)
            parts.append(_dot(c, zr_ref[rows, :].astype(BF16)) + _dot(s, zi_ref[rows, :].astype(BF16)))
        yf = jnp.concatenate(parts, axis=0) * (1.0 / np.sqrt(dense_len * GROUP))
    else:
        ya_ref, yf_ref, zb_ref, ga_ref, gb_ref, x_ref = refs[:6]
        pos = 6
        yf = yf_ref[...]
    x = x_ref[...]
    if has_pe:
        x = x + refs[pos][...]
        pos += 1
    mod_ref, wpa_ref, wpb_ref, wo_ref = refs[pos:pos + 4]
    pos += 4
    if final:
        fw_ref = refs[pos]
        pos += 1
    out_ref = refs[pos]
    yb = (yf * zb_ref[...].astype(F32)).astype(BF16)
    merged = (ga_ref[...].astype(F32) * _dot(ya_ref[...], wpa_ref[...])
              + gb_ref[...].astype(F32) * _dot(yb, wpb_ref[...]))
    out = _dot(merged.astype(BF16), wo_ref[...])
    xn = x + mod_ref[:, 2 * D:3 * D] * out
    if final:
        ms = jnp.mean(xn * xn, axis=-1, keepdims=True)
        xn = xn * lax.rsqrt(ms + EPS) * fw_ref[...]
    out_ref[...] = xn


def _merge(ya, yf, zb, pa, x, pe, mod, wpa, wpb, wo, final_w, layer, seq_len, latent):
    tokens = x.shape[0]
    tm = TM_MERGE
    has_pe = pe is not None
    final = final_w is not None
    dense_len = seq_len if isinstance(yf, tuple) else 0
    row = lambda i: (i, 0)
    if latent:
        mod_idx = lambda i: (layer * MOD_ROWS + 1 + (i * tm) // seq_len, 0, 0)
    else:
        mod_idx = lambda i: (layer * MOD_ROWS, 0, 0)
    in_specs = [pl.BlockSpec((tm, D_A), row)]
    args = [ya]
    if dense_len:
        c, s = _dft_cos_sin(seq_len)
        mat = pl.BlockSpec((seq_len, seq_len), lambda i: (0, 0))
        in_specs += [pl.BlockSpec((tm, D_B), row), pl.BlockSpec((tm, D_B), row), mat, mat]
        args += [yf[0], yf[1], jnp.asarray(c, F32), jnp.asarray(s, F32)]
    else:
        in_specs.append(pl.BlockSpec((tm, D_B), row))
        args.append(yf)
    in_specs += [
        pl.BlockSpec((tm, D_B), row),
        pl.BlockSpec((tm, D), lambda i: (i, PA_GA)),
        pl.BlockSpec((tm, D), lambda i: (i, PA_GB)),
        pl.BlockSpec((tm, D), row),
    ]
    args += [zb, pa, pa, x]
    if has_pe:
        per_seq = seq_len // tm
        in_specs.append(pl.BlockSpec((tm, D), lambda i: (i % per_seq, 0)))
        args.append(pe)
    in_specs += [
        pl.BlockSpec((None, 1, 3 * D), mod_idx),
        pl.BlockSpec((None, D_A, D), lambda i: (layer, 0, 0)),
        pl.BlockSpec((None, D_B, D), lambda i: (layer, 0, 0)),
        pl.BlockSpec((None, D, D), lambda i: (layer, 0, 0)),
    ]
    args += [mod, wpa, wpb, wo]
    if final:
        in_specs.append(pl.BlockSpec((1, D), lambda i: (0, 0)))
        args.append(final_w)
    return pl.pallas_call(
        functools.partial(_merge_kernel, has_pe, final, dense_len),
        grid=(tokens // tm,),
        in_specs=in_specs,
        out_specs=pl.BlockSpec((tm, D), row),
        out_shape=jax.ShapeDtypeStruct((tokens, D), F32),
        compiler_params=_cparams(1),
        name="merge",
    )(*args)


def kernel(x_prompt, x_sample, state_hgrn, c, c_ctx, norm_w, w_ada, b_ada, w_in, lb_raw,
           gnorm_w, w_pa, w_pb, w_o, final_norm_w):
    n_p, len_p, _ = x_prompt.shape
    n_s, len_s, _ = x_sample.shape

    cond = jnp.zeros((MOD_ROWS, D), F32).at[0].set(c_ctx).at[1:1 + n_s].set(c)
    mod = _modulation(cond, w_ada, b_ada)
    lbs = _lower_bounds(lb_raw)

    w_in_bf = w_in.astype(BF16)
    wpa_bf = w_pa.astype(BF16)
    wpb_bf = w_pb.astype(BF16)
    wo_bf = w_o.astype(BF16)
    norm_w3 = norm_w.reshape(DEPTH, 1, D)
    gnorm3 = gnorm_w.reshape(DEPTH, 1, D_A)
    final_w = final_norm_w.reshape(1, D)

    cc_np, sc_np = _dft_cos_sin(GROUP)
    cc = jnp.asarray(cc_np, F32)
    sc = jnp.asarray(sc_np, F32)
    pe = jnp.asarray(_pos_embed_table(len_s, D))

    xp = x_prompt.reshape(n_p * len_p, D)
    xs = x_sample.reshape(n_s * len_s, D)
    new_state = None
    for l in range(DEPTH):
        last = l == DEPTH - 1
        fw = final_w if last else None
        pe_l = pe if l == 0 else None

        pf, pa, zr, zi, zb = _proj(xp, None, mod, norm_w3, lbs, w_in_bf, cc, sc, l, len_p, False)
        ya, new_state = _scan(pf, pa, gnorm3, None, l, n_p, len_p, True, new_state)
        xp = _merge(ya, (zr, zi), zb, pa, xp, None, mod, wpa_bf, wpb_bf, wo_bf, fw, l, len_p, False)

        pf, pa, zr, zi, zb = _proj(xs, pe_l, mod, norm_w3, lbs, w_in_bf, cc, sc, l, len_s, True)
        ya, _ = _scan(pf, pa, gnorm3, state_hgrn, l, n_s, len_s, False)
        yf = _fourier_two_stage(zr, zi, n_s, len_s)
        xs = _merge(ya, yf, zb, pa, xs, pe_l, mod, wpa_bf, wpb_bf, wo_bf, fw, l, len_s, True)

    y_prompt = xp.reshape(n_p, len_p, D)
    y_sample = xs.reshape(n_s, len_s, D)
    return (y_prompt, y_sample, new_state)
```

```python
import functools

import jax
import jax.numpy as jnp
import numpy as np
from jax import lax
from jax.experimental import pallas as pl
from jax.experimental.pallas import tpu as pltpu

D = 1024
DEPTH = 4
N_HEADS = 8
HEAD = 128
D_A = N_HEADS * HEAD
N_GROUPS = 4
GROUP = 128
D_B = N_GROUPS * GROUP
D_IN = 5 * D_A + 2 * D_B + 2 * D
GRID_W = 64
EPS = 1e-6

V7X_VMEM_LIMIT_BYTES = 56 * 1024 * 1024

SCAN_CHUNK = 64
SCAN_GROUP = 16
SCAN_UNROLL = 4
SCAN_SHORT_SEQ = 512
SCAN_HEADS_SHORT = 4
EXP_CLAMP = 80.0
TM_PROJ = 256
CUM_ROWS = 256
TM_MERGE = 512
FFT_RADIX = 64
FFT_BLOCK = 32
MOD_ROWS = 8

F32 = jnp.float32
BF16 = jnp.bfloat16


def _cparams(n_axes):
    return pltpu.CompilerParams(
        dimension_semantics=("arbitrary",) * n_axes,
        vmem_limit_bytes=V7X_VMEM_LIMIT_BYTES,
    )


def _dot(a, b):
    return jnp.dot(a, b, preferred_element_type=F32)


def _dot_nt(a, b):
    return lax.dot_general(a, b, (((1,), (1,)), ((), ())), preferred_element_type=F32)


def _dot_tn(a, b):
    return lax.dot_general(a, b, (((0,), (0,)), ((), ())), preferred_element_type=F32)


def _sigmoid(x):
    return 1.0 / (1.0 + jnp.exp(-x))


def _silu(x):
    return x * _sigmoid(x)


def _dft_cos_sin(n):
    k = np.arange(n, dtype=np.float64)
    ang = 2.0 * np.pi * np.outer(k, k) / n
    return np.cos(ang), np.sin(ang)


def _pos_embed_table(length, d):
    rows = length // GRID_W
    r = np.repeat(np.arange(rows, dtype=np.float64), GRID_W)
    col = np.tile(np.arange(GRID_W, dtype=np.float64), rows)
    nf = d // 4
    freqs = 1.0 / (10000.0 ** (np.arange(nf, dtype=np.float64) / nf))

    def emb(p):
        a = p[:, None] * freqs[None, :]
        return np.concatenate([np.sin(a), np.cos(a)], axis=-1)

    return np.concatenate([emb(r), emb(col)], axis=-1).astype(np.float32)


def _lb_kernel(raw_ref, out_ref):
    x = raw_ref[...]
    m = jnp.max(x, axis=0, keepdims=True)
    e = jnp.exp(x - m)
    p = e / jnp.sum(e, axis=0, keepdims=True)
    cs = p[0:1]
    first = cs
    out_ref[0:1, :] = cs - first
    for l in range(1, DEPTH):
        cs = cs + p[l:l + 1]
        out_ref[l:l + 1, :] = cs - first


def _lower_bounds(lb_raw):
    raw = lb_raw.reshape(DEPTH, 2 * D_A)
    out = pl.pallas_call(
        _lb_kernel,
        out_shape=jax.ShapeDtypeStruct((DEPTH, 2 * D_A), F32),
        name="lower_bounds",
    )(raw)
    return out.reshape(DEPTH * 2, 1, D_A)


def _mod_kernel(cond_ref, w_ref, b_ref, out_ref):
    a = _silu(cond_ref[...])
    out_ref[...] = jnp.dot(a, w_ref[...], preferred_element_type=F32,
                           precision=lax.Precision.HIGHEST) + b_ref[...]


def _modulation(cond, w_ada, b_ada):
    tn = 512
    out = pl.pallas_call(
        _mod_kernel,
        grid=(DEPTH, 3 * D // tn),
        in_specs=[
            pl.BlockSpec((MOD_ROWS, D), lambda l, j: (0, 0)),
            pl.BlockSpec((None, D, tn), lambda l, j: (l, 0, j)),
            pl.BlockSpec((None, 1, tn), lambda l, j: (l, 0, j)),
        ],
        out_specs=pl.BlockSpec((None, MOD_ROWS, tn), lambda l, j: (l, 0, j)),
        out_shape=jax.ShapeDtypeStruct((DEPTH, MOD_ROWS, 3 * D), F32),
        compiler_params=_cparams(2),
        name="modulation",
    )(cond, w_ada, b_ada.reshape(DEPTH, 1, 3 * D))
    return out.reshape(DEPTH * MOD_ROWS, 1, 3 * D)


_Q0, _FF0, _V0, _ZA0, _U0, _ZB0, _GA0, _GB0 = (
    0, D_A, 3 * D_A, 4 * D_A, 5 * D_A, 5 * D_A + D_B, 5 * D_A + 2 * D_B, 5 * D_A + 2 * D_B + D)
PA_Q, PA_V, PA_Z, PA_GA, PA_GB, PA_KF = 0, 1, 2, 3, 4, 5
PA_WIDTH = 7 * D


def _forget_gate(x, lb):
    f = lb + (1.0 - lb) * _sigmoid(x)
    log_f = jnp.where(f < 1e-30, jnp.minimum(x, 0.0), jnp.log(jnp.maximum(f, 1e-30)))
    return log_f, 1.0 - f


def _proj_kernel(has_pe, *refs):
    refs = list(refs)
    x = refs[0][...]
    pos = 1
    if has_pe:
        x = x + refs[pos][...]
        pos += 1
    mod_ref, nw_ref, lbf_ref, lbb_ref, w_ref, cc_ref, sc_ref = refs[pos:pos + 7]
    pf_ref, pa_ref, zr_ref, zi_ref, zb_ref = refs[pos + 7:]

    ms = jnp.mean(x * x, axis=-1, keepdims=True)
    y = x * lax.rsqrt(ms + EPS) * nw_ref[...]
    shift = mod_ref[:, 0:D]
    scale = mod_ref[:, D:2 * D]
    h = (y * (1.0 + scale) + shift).astype(BF16)

    def sec(c0, width):
        return _dot(h, w_ref[:, c0:c0 + width])

    def put(block, val):
        pa_ref[:, block * D:(block + 1) * D] = val.astype(BF16)

    row = lax.broadcasted_iota(jnp.int32, (CUM_ROWS, CUM_ROWS), 0)
    col = lax.broadcasted_iota(jnp.int32, (CUM_ROWS, CUM_ROWS), 1)
    same_chunk = (row // SCAN_CHUNK) == (col // SCAN_CHUNK)
    tris = (jnp.where(same_chunk & (col <= row), 1.0, 0.0).astype(BF16),
            jnp.where(same_chunk & (col >= row), 1.0, 0.0).astype(BF16))
    for d, lb_ref in enumerate((lbf_ref, lbb_ref)):
        g, kk = _forget_gate(sec(_FF0 + d * D_A, D_A), lb_ref[...])
        g_hi = g.astype(BF16)
        g_lo = (g - g_hi.astype(F32)).astype(BF16)
        for r0 in range(0, x.shape[0], CUM_ROWS):
            rows = slice(r0, r0 + CUM_ROWS)
            pf_ref[rows, d * D_A:(d + 1) * D_A] = _dot(tris[d], g_hi[rows]) + _dot(tris[d], g_lo[rows])
        put(PA_KF + d, kk)

    put(PA_Q, _silu(sec(_Q0, D_A)))
    put(PA_V, sec(_V0, D_A))
    put(PA_Z, _silu(sec(_ZA0, D_A)))
    put(PA_GA, _sigmoid(sec(_GA0, D)))
    put(PA_GB, _sigmoid(sec(_GB0, D)))
    zb_ref[...] = _silu(sec(_ZB0, D_B)).astype(BF16)
    u = sec(_U0, D_B).astype(BF16)
    cc = cc_ref[...].astype(BF16)
    sc = sc_ref[...].astype(BF16)
    for g in range(N_GROUPS):
        ug = u[:, g * GROUP:(g + 1) * GROUP]
        zr_ref[:, g * GROUP:(g + 1) * GROUP] = _dot(ug, cc)
        zi_ref[:, g * GROUP:(g + 1) * GROUP] = -_dot(ug, sc)


def _proj(x, pe, mod, norm_w3, lbs, w_in_bf, cc, sc, layer, seq_len, latent):
    tokens = x.shape[0]
    tm = TM_PROJ
    has_pe = pe is not None
    per_seq = seq_len // tm
    if latent:
        mod_idx = lambda i: (layer * MOD_ROWS + 1 + i // per_seq, 0, 0)
    else:
        mod_idx = lambda i: (layer * MOD_ROWS, 0, 0)
    in_specs = [pl.BlockSpec((tm, D), lambda i: (i, 0))]
    args = [x]
    if has_pe:
        in_specs.append(pl.BlockSpec((tm, D), lambda i: (i % per_seq, 0)))
        args.append(pe)
    in_specs += [
        pl.BlockSpec((None, 1, 3 * D), mod_idx),
        pl.BlockSpec((None, 1, D), lambda i: (layer, 0, 0)),
        pl.BlockSpec((None, 1, D_A), lambda i: (2 * layer, 0, 0)),
        pl.BlockSpec((None, 1, D_A), lambda i: (2 * layer + 1, 0, 0)),
        pl.BlockSpec((None, D, D_IN), lambda i: (layer, 0, 0), pipeline_mode=pl.Buffered(1)),
        pl.BlockSpec((GROUP, GROUP), lambda i: (0, 0)),
        pl.BlockSpec((GROUP, GROUP), lambda i: (0, 0)),
    ]
    args += [mod, norm_w3, lbs, lbs, w_in_bf, cc, sc]
    out_shape = (
        jax.ShapeDtypeStruct((tokens, 2 * D_A), F32),
        jax.ShapeDtypeStruct((tokens, PA_WIDTH), BF16),
        jax.ShapeDtypeStruct((tokens, D_B), F32),
        jax.ShapeDtypeStruct((tokens, D_B), F32),
        jax.ShapeDtypeStruct((tokens, D_B), BF16),
    )
    out_specs = (
        pl.BlockSpec((tm, 2 * D_A), lambda i: (i, 0)),
        pl.BlockSpec((tm, PA_WIDTH), lambda i: (i, 0)),
        pl.BlockSpec((tm, D_B), lambda i: (i, 0)),
        pl.BlockSpec((tm, D_B), lambda i: (i, 0)),
        pl.BlockSpec((tm, D_B), lambda i: (i, 0)),
    )
    return pl.pallas_call(
        functools.partial(_proj_kernel, has_pe),
        grid=(tokens // tm,),
        in_specs=in_specs,
        out_specs=out_specs,
        out_shape=out_shape,
        compiler_params=_cparams(1),
        name="proj",
    )(*args)


def _scan_kernel(seq_len, heads, has_init, emit_state, has_acc, *refs):
    refs = list(refs)
    bf_ref, bb_ref, q_ref, v_ref, z_ref, kf_ref, kb_ref, gn_ref = refs[:8]
    pos = 8
    init_ref = None
    if has_init:
        init_ref = refs[pos]
        pos += 1
    if has_acc:
        pos += 1
    y_ref = refs[pos]
    pos += 1
    st_ref = None
    if emit_state:
        st_ref = refs[pos]
        pos += 1
    (o_ref, sf_ref, sb_ref, qinf_ref, qinb_ref, uf_ref, ub_ref, elf_ref, elb_ref) = refs[pos:pos + 9]

    C = SCAN_CHUNK
    H = C // 2
    n_chunks = seq_len // C
    G = min(SCAN_GROUP, n_chunks)
    GC = G * C
    n_groups = seq_len // GC
    lanes = [slice(hd * HEAD, (hd + 1) * HEAD) for hd in range(heads)]

    def pair_masks(n):
        rr = lax.broadcasted_iota(jnp.int32, (n, n), 0)
        cc = lax.broadcasted_iota(jnp.int32, (n, n), 1)
        return cc <= rr, cc >= rr

    mask_c = pair_masks(C)
    mask_h = pair_masks(H)
    dirs = (
        (bf_ref, kf_ref, C - 1, qinf_ref, uf_ref, elf_ref),
        (bb_ref, kb_ref, 0, qinb_ref, ub_ref, elb_ref),
    )

    def split_factors(bq, qv, kv):
        n = bq.shape[0]
        r = 0.5 * (bq[0:1, :] + bq[n - 1:n, :])
        e = jnp.exp(jnp.clip(bq - r, -EXP_CLAMP, EXP_CLAMP))
        return qv * e, kv * (1.0 / e), r, jnp.abs(bq[0:1, :] - r)

    def halves(d, bc, qc, kc, vc):
        first, second = (slice(0, H), slice(H, C)) if d == 0 else (slice(H, C), slice(0, H))
        edge = bc[H - 1:H, :] if d == 0 else bc[H:H + 1, :]
        atts, dev = [], None
        for rows in (first, second):
            qt, kt, _, dev_h = split_factors(bc[rows], qc[rows], kc[rows])
            att = _dot_nt(qt.astype(BF16), kt.astype(BF16))
            atts.append(jnp.where(mask_h[d], att, 0.0).astype(BF16))
            dev = dev_h if dev is None else jnp.maximum(dev, dev_h)
        cross = _dot_nt((qc[second] * jnp.exp(bc[second] - edge)).astype(BF16),
                        (kc[first] * jnp.exp(edge - bc[first])).astype(BF16)).astype(BF16)
        o_first = _dot(atts[0], vc[first])
        o_second = _dot(atts[1], vc[second]) + _dot(cross, vc[first])
        parts = [o_first, o_second] if d == 0 else [o_second, o_first]
        return jnp.concatenate(parts, axis=0), dev

    def pairwise(d, bc, qc, kc, vf):
        s_idx = lax.broadcasted_iota(jnp.int32, (C, 1), 0)

        def row(t, o_blk):
            sel = s_idx == t
            b_t = jnp.sum(jnp.where(sel, bc, 0.0), axis=0, keepdims=True)
            q_t = jnp.sum(jnp.where(sel, qc, 0.0), axis=0, keepdims=True)
            allowed = (s_idx <= t) if d == 0 else (s_idx >= t)
            decay = jnp.where(allowed, jnp.exp(jnp.where(allowed, b_t - bc, 0.0)), 0.0)
            att_t = jnp.sum(decay * (q_t * kc), axis=1, keepdims=True)
            o_t = jnp.sum(att_t * vf, axis=0, keepdims=True)
            return jnp.where(sel, o_t, o_blk)

        return lax.fori_loop(0, C, row, jnp.zeros((C, HEAD), F32))

    def redo_chunk(hd, chunk, rows):
        ls = lanes[hd]
        bcs = [b_ref[rows, ls] for b_ref, _, _, _, _, _ in dirs]
        span = None
        for bc in bcs:
            r = 0.5 * (bc[0:1, :] + bc[C - 1:C, :])
            dev = jnp.maximum(jnp.abs(bc[0:1, :] - r), jnp.abs(r))
            span = dev if span is None else jnp.maximum(span, dev)

        @pl.when(jnp.max(span) > EXP_CLAMP)
        def _():
            qc = q_ref[rows, ls].astype(F32)
            vc = v_ref[rows, ls]
            kcs = [k_ref[rows, ls].astype(F32) for _, k_ref, _, _, _, _ in dirs]
            o_sum, span_h = None, None
            for d, (_, _, last_row, qin_ref, u_ref, _) in enumerate(dirs):
                bc, kc = bcs[d], kcs[d]
                b_last = bc[last_row:last_row + 1, :]
                qin_ref[rows, ls] = (qc * jnp.exp(bc)).astype(BF16)
                u_ref[hd * n_chunks + chunk] = _dot_tn(vc, (kc * jnp.exp(b_last - bc)).astype(BF16))
                o_d, dev = halves(d, bc, qc, kc, vc)
                o_sum = o_d if o_sum is None else o_sum + o_d
                span_h = dev if span_h is None else jnp.maximum(span_h, dev)
            o_ref[rows, ls] = o_sum

            @pl.when(jnp.max(span_h) > EXP_CLAMP)
            def _():
                vf = vc.astype(F32)
                o_ref[rows, ls] = pairwise(0, bcs[0], qc, kcs[0], vf) + pairwise(1, bcs[1], qc, kcs[1], vf)

    def local(j, carry):
        r0 = pl.multiple_of(j * GC, GC)
        work = []
        span = None
        for hd, ls in enumerate(lanes):
            q = q_ref[pl.ds(r0, GC), ls].astype(F32)
            v = v_ref[pl.ds(r0, GC), ls]
            for d, (b_ref, k_ref, last_row, qin_ref, u_ref, el_ref) in enumerate(dirs):
                b = b_ref[pl.ds(r0, GC), ls]
                kk = k_ref[pl.ds(r0, GC), ls].astype(F32)
                for c in range(G):
                    sl = slice(c * C, (c + 1) * C)
                    bc = b[sl]
                    b_last = bc[last_row:last_row + 1, :]
                    qt, kt, r, dev = split_factors(bc, q[sl], kk[sl])
                    dev = jnp.maximum(dev, jnp.abs(r))
                    span = dev if span is None else jnp.maximum(span, dev)
                    qin_ref[pl.ds(r0 + c * C, C), ls] = (qt * jnp.exp(r)).astype(BF16)
                    el_ref[hd * n_chunks + j * G + c] = jnp.exp(b_last)
                    k_out = (kt * jnp.exp(b_last - r)).astype(BF16)
                    work.append((qt.astype(BF16), kt.astype(BF16), k_out, mask_c[d], v[sl], u_ref,
                                 hd * n_chunks + j * G + c))
        atts = [_dot_nt(qt, kt) for qt, kt, _, _, _, _, _ in work]
        atts = [jnp.where(w[3], a, 0.0).astype(BF16) for w, a in zip(work, atts)]
        for (_, _, k_out, _, vc, u_ref, slot) in work:
            u_ref[slot] = _dot_tn(vc, k_out)
        outs = [_dot(a, w[4]) for w, a in zip(work, atts)]
        for hd, ls in enumerate(lanes):
            o_f = jnp.concatenate(outs[(2 * hd) * G:(2 * hd + 1) * G], axis=0)
            o_b = jnp.concatenate(outs[(2 * hd + 1) * G:(2 * hd + 2) * G], axis=0)
            o_ref[pl.ds(r0, GC), ls] = o_f + o_b

        @pl.when(jnp.max(span) > EXP_CLAMP)
        def _():
            for hd in range(heads):
                def redo(c, carry, hd=hd):
                    redo_chunk(hd, j * G + c, pl.ds(pl.multiple_of(j * GC + c * C, C), C))
                    return carry

                lax.fori_loop(0, G, redo, 0)

        return carry

    lax.fori_loop(0, n_groups, local, 0)

    for hd in range(heads):
        base = hd * n_chunks
        if has_init:
            s_init = (init_ref[0, hd].T, init_ref[1, hd].T)
        else:
            s_init = (jnp.zeros((HEAD, HEAD), F32), jnp.zeros((HEAD, HEAD), F32))

        def step(i, carry, base=base):
            s_f, s_b = carry
            kf = base + i
            kb = base + n_chunks - 1 - i
            sf_ref[kf] = s_f.astype(BF16)
            sb_ref[kb] = s_b.astype(BF16)
            return (s_f * elf_ref[kf] + uf_ref[kf], s_b * elb_ref[kb] + ub_ref[kb])

        s_f, s_b = lax.fori_loop(0, n_chunks, step, s_init, unroll=SCAN_UNROLL)
        if emit_state:
            st_ref[0, hd] = s_f.T
            st_ref[1, hd] = s_b.T

    def add_state_readout(j):
        r0 = pl.multiple_of(j * GC, GC)
        for hd, ls in enumerate(lanes):
            o = o_ref[pl.ds(r0, GC), ls]
            for qin_ref, s_ref in ((qinf_ref, sf_ref), (qinb_ref, sb_ref)):
                o = o + jnp.concatenate(
                    [_dot_nt(qin_ref[pl.ds(r0 + c * C, C), ls], s_ref[hd * n_chunks + j * G + c])
                     for c in range(G)], axis=0)
            o_ref[pl.ds(r0, GC), ls] = o

    def normalise(j):
        r0 = pl.multiple_of(j * GC, GC)
        for ls in lanes:
            o = o_ref[pl.ds(r0, GC), ls]
            ms = jnp.mean(o * o, axis=-1, keepdims=True)
            o = o * lax.rsqrt(ms + EPS) * gn_ref[:, ls]
            y_ref[pl.ds(r0, GC), ls] = (o * z_ref[pl.ds(r0, GC), ls].astype(F32)).astype(BF16)

    add_state_readout(0)

    def finish(j, carry):
        normalise(j - 1)
        add_state_readout(j)
        return carry

    lax.fori_loop(1, n_groups, finish, 0)
    normalise(n_groups - 1)


def _scan(pf, pa, gnorm3, init_state, layer, n_seq, seq_len, emit_state, state_acc=None):
    tokens = n_seq * seq_len
    n_chunks = seq_len // SCAN_CHUNK
    has_init = init_state is not None
    has_acc = state_acc is not None
    heads = SCAN_HEADS_SHORT if seq_len <= SCAN_SHORT_SEQ else 1
    width = heads * HEAD
    blk = lambda c0: pl.BlockSpec((seq_len, width), lambda b, h, c0=c0: (b, c0 // heads + h))
    in_specs = [
        blk(0), blk(N_HEADS),
        blk(PA_Q * N_HEADS), blk(PA_V * N_HEADS), blk(PA_Z * N_HEADS),
        blk(PA_KF * N_HEADS), blk((PA_KF + 1) * N_HEADS),
        pl.BlockSpec((None, 1, width), lambda b, h: (layer, 0, h)),
    ]
    args = [pf, pf, pa, pa, pa, pa, pa, gnorm3]
    if has_init:
        in_specs.append(pl.BlockSpec((None, None, 2, heads, HEAD, HEAD),
                                     lambda b, h: (b, layer, 0, h, 0, 0)))
        args.append(init_state)
    aliases = {}
    if has_acc:
        aliases = {len(args): 1}
        in_specs.append(pl.BlockSpec(memory_space=pl.ANY))
        args.append(state_acc)
    out_shape = [jax.ShapeDtypeStruct((tokens, D_A), BF16)]
    out_specs = [pl.BlockSpec((seq_len, width), lambda b, h: (b, h))]
    if emit_state:
        out_shape.append(jax.ShapeDtypeStruct((n_seq, DEPTH, 2, N_HEADS, HEAD, HEAD), F32))
        out_specs.append(pl.BlockSpec((None, None, 2, heads, HEAD, HEAD),
                                      lambda b, h: (b, layer, 0, h, 0, 0)))
    slots = heads * n_chunks
    res = pl.pallas_call(
        functools.partial(_scan_kernel, seq_len, heads, has_init, emit_state, has_acc),
        grid=(n_seq, N_HEADS // heads),
        in_specs=in_specs,
        out_specs=out_specs,
        out_shape=out_shape,
        input_output_aliases=aliases,
        scratch_shapes=[
            pltpu.VMEM((seq_len, width), F32),
            pltpu.VMEM((slots, HEAD, HEAD), BF16), pltpu.VMEM((slots, HEAD, HEAD), BF16),
            pltpu.VMEM((seq_len, width), BF16), pltpu.VMEM((seq_len, width), BF16),
            pltpu.VMEM((slots, HEAD, HEAD), F32), pltpu.VMEM((slots, HEAD, HEAD), F32),
            pltpu.VMEM((slots, 1, HEAD), F32), pltpu.VMEM((slots, 1, HEAD), F32),
        ],
        compiler_params=_cparams(2),
        name="scan",
    )(*args)
    return res if emit_state else (res[0], None)


def _fourier_stage1_kernel(zr_ref, zi_ref, m_ref, tc_ref, ts_ref, br_ref, bi_ref):
    m = m_ref[...].astype(BF16)
    for n in range(FFT_BLOCK):
        z = jnp.concatenate([zr_ref[:, n, :], zi_ref[:, n, :]], axis=0).astype(BF16)
        a = _dot(m, z)
        ar = a[:FFT_RADIX]
        ai = a[FFT_RADIX:]
        tc = jnp.concatenate([tc_ref[n]] * N_GROUPS, axis=1)
        ts = jnp.concatenate([ts_ref[n]] * N_GROUPS, axis=1)
        br_ref[:, n, :] = ar * tc + ai * ts
        bi_ref[:, n, :] = ai * tc - ar * ts


def _fourier_stage2_kernel(scale, br_ref, bi_ref, cs_ref, y_ref):
    cs = cs_ref[...].astype(BF16)
    for j in range(FFT_BLOCK):
        bcat = jnp.concatenate([br_ref[j], bi_ref[j]], axis=0).astype(BF16)
        y_ref[:, j, :] = _dot(cs, bcat) * scale


def _fourier_two_stage(zr, zi, n_seq, seq_len):
    R = FFT_RADIX
    assert seq_len == R * R
    c, s = _dft_cos_sin(R)
    m1 = np.block([[c, s], [-s, c]])
    idx = np.arange(R, dtype=np.float64)
    ang = 2.0 * np.pi * np.outer(idx, idx) / seq_len
    tc = np.repeat(np.cos(ang)[:, :, None], GROUP, axis=2).astype(np.float32)
    ts = np.repeat(np.sin(ang)[:, :, None], GROUP, axis=2).astype(np.float32)
    cs = np.concatenate([c, s], axis=1)
    scale = 1.0 / np.sqrt(seq_len * GROUP)

    fast_blk = pl.BlockSpec((None, R, FFT_BLOCK, D_B), lambda b, j: (b, 0, j, 0))
    slow_blk = pl.BlockSpec((None, FFT_BLOCK, R, D_B), lambda b, j: (b, j, 0, 0))
    tw_blk = pl.BlockSpec((FFT_BLOCK, R, GROUP), lambda b, j: (j, 0, 0))
    shape4 = jax.ShapeDtypeStruct((n_seq, R, R, D_B), F32)
    br, bi = pl.pallas_call(
        _fourier_stage1_kernel,
        grid=(n_seq, R // FFT_BLOCK),
        in_specs=[fast_blk, fast_blk, pl.BlockSpec((2 * R, 2 * R), lambda b, j: (0, 0)), tw_blk, tw_blk],
        out_specs=(fast_blk, fast_blk),
        out_shape=(shape4, shape4),
        compiler_params=_cparams(2),
        name="fourier_stage1",
    )(zr.reshape(n_seq, R, R, D_B), zi.reshape(n_seq, R, R, D_B),
      jnp.asarray(m1, F32), jnp.asarray(tc), jnp.asarray(ts))

    y = pl.pallas_call(
        functools.partial(_fourier_stage2_kernel, scale),
        grid=(n_seq, R // FFT_BLOCK),
        in_specs=[slow_blk, slow_blk, pl.BlockSpec((R, 2 * R), lambda b, j: (0, 0))],
        out_specs=fast_blk,
        out_shape=shape4,
        compiler_params=_cparams(2),
        name="fourier_stage2",
    )(br, bi, jnp.asarray(cs, F32))
    return y.reshape(n_seq * seq_len, D_B)


def _merge_kernel(has_pe, final, dense_len, *refs):
    refs = list(refs)
    if dense_len:
        ya_ref, zr_ref, zi_ref, c_ref, s_ref, zb_ref, ga_ref, gb_ref, x_ref = refs[:9]
        pos = 9
        c = c_ref[...].astype(BF16)
        s = s_ref[...].astype(BF16)
        parts = []
        for r0 in range(0, x_ref.shape[0], dense_len):
            rows = slice(r0, r0 + dense_len)
            parts.append(_dot(c, zr_ref[rows, :].astype(BF16)) + _dot(s, zi_ref[rows, :].astype(BF16)))
        yf = jnp.concatenate(parts, axis=0) * (1.0 / np.sqrt(dense_len * GROUP))
    else:
        ya_ref, yf_ref, zb_ref, ga_ref, gb_ref, x_ref = refs[:6]
        pos = 6
        yf = yf_ref[...]
    x = x_ref[...]
    if has_pe:
        x = x + refs[pos][...]
        pos += 1
    mod_ref, wpa_ref, wpb_ref, wo_ref = refs[pos:pos + 4]
    pos += 4
    if final:
        fw_ref = refs[pos]
        pos += 1
    out_ref = refs[pos]
    yb = (yf * zb_ref[...].astype(F32)).astype(BF16)
    merged = (ga_ref[...].astype(F32) * _dot(ya_ref[...], wpa_ref[...])
              + gb_ref[...].astype(F32) * _dot(yb, wpb_ref[...]))
    out = _dot(merged.astype(BF16), wo_ref[...])
    xn = x + mod_ref[:, 2 * D:3 * D] * out
    if final:
        ms = jnp.mean(xn * xn, axis=-1, keepdims=True)
        xn = xn * lax.rsqrt(ms + EPS) * fw_ref[...]
    out_ref[...] = xn


def _merge(ya, yf, zb, pa, x, pe, mod, wpa, wpb, wo, final_w, layer, seq_len, latent):
    tokens = x.shape[0]
    tm = TM_MERGE
    has_pe = pe is not None
    final = final_w is not None
    dense_len = seq_len if isinstance(yf, tuple) else 0
    row = lambda i: (i, 0)
    if latent:
        mod_idx = lambda i: (layer * MOD_ROWS + 1 + (i * tm) // seq_len, 0, 0)
    else:
        mod_idx = lambda i: (layer * MOD_ROWS, 0, 0)
    in_specs = [pl.BlockSpec((tm, D_A), row)]
    args = [ya]
    if dense_len:
        c, s = _dft_cos_sin(seq_len)
        mat = pl.BlockSpec((seq_len, seq_len), lambda i: (0, 0))
        in_specs += [pl.BlockSpec((tm, D_B), row), pl.BlockSpec((tm, D_B), row), mat, mat]
        args += [yf[0], yf[1], jnp.asarray(c, F32), jnp.asarray(s, F32)]
    else:
        in_specs.append(pl.BlockSpec((tm, D_B), row))
        args.append(yf)
    in_specs += [
        pl.BlockSpec((tm, D_B), row),
        pl.BlockSpec((tm, D), lambda i: (i, PA_GA)),
        pl.BlockSpec((tm, D), lambda i: (i, PA_GB)),
        pl.BlockSpec((tm, D), row),
    ]
    args += [zb, pa, pa, x]
    if has_pe:
        per_seq = seq_len // tm
        in_specs.append(pl.BlockSpec((tm, D), lambda i: (i % per_seq, 0)))
        args.append(pe)
    in_specs += [
        pl.BlockSpec((None, 1, 3 * D), mod_idx),
        pl.BlockSpec((None, D_A, D), lambda i: (layer, 0, 0)),
        pl.BlockSpec((None, D_B, D), lambda i: (layer, 0, 0)),
        pl.BlockSpec((None, D, D), lambda i: (layer, 0, 0)),
    ]
    args += [mod, wpa, wpb, wo]
    if final:
        in_specs.append(pl.BlockSpec((1, D), lambda i: (0, 0)))
        args.append(final_w)
    return pl.pallas_call(
        functools.partial(_merge_kernel, has_pe, final, dense_len),
        grid=(tokens // tm,),
        in_specs=in_specs,
        out_specs=pl.BlockSpec((tm, D), row),
        out_shape=jax.ShapeDtypeStruct((tokens, D), F32),
        compiler_params=_cparams(1),
        name="merge",
    )(*args)


def kernel(x_prompt, x_sample, state_hgrn, c, c_ctx, norm_w, w_ada, b_ada, w_in, lb_raw,
           gnorm_w, w_pa, w_pb, w_o, final_norm_w):
    n_p, len_p, _ = x_prompt.shape
    n_s, len_s, _ = x_sample.shape

    cond = jnp.zeros((MOD_ROWS, D), F32).at[0].set(c_ctx).at[1:1 + n_s].set(c)
    mod = _modulation(cond, w_ada, b_ada)
    lbs = _lower_bounds(lb_raw)

    w_in_bf = w_in.astype(BF16)
    wpa_bf = w_pa.astype(BF16)
    wpb_bf = w_pb.astype(BF16)
    wo_bf = w_o.astype(BF16)
    norm_w3 = norm_w.reshape(DEPTH, 1, D)
    gnorm3 = gnorm_w.reshape(DEPTH, 1, D_A)
    final_w = final_norm_w.reshape(1, D)

    cc_np, sc_np = _dft_cos_sin(GROUP)
    cc = jnp.asarray(cc_np, F32)
    sc = jnp.asarray(sc_np, F32)
    pe = jnp.asarray(_pos_embed_table(len_s, D))

    xp = x_prompt.reshape(n_p * len_p, D)
    xs = x_sample.reshape(n_s * len_s, D)
    new_state = None
    for l in range(DEPTH):
        last = l == DEPTH - 1
        fw = final_w if last else None
        pe_l = pe if l == 0 else None

        pf, pa, zr, zi, zb = _proj(xp, None, mod, norm_w3, lbs, w_in_bf, cc, sc, l, len_p, False)
        ya, new_state = _scan(pf, pa, gnorm3, None, l, n_p, len_p, True, new_state)
        xp = _merge(ya, (zr, zi), zb, pa, xp, None, mod, wpa_bf, wpb_bf, wo_bf, fw, l, len_p, False)

        pf, pa, zr, zi, zb = _proj(xs, pe_l, mod, norm_w3, lbs, w_in_bf, cc, sc, l, len_s, True)
        ya, _ = _scan(pf, pa, gnorm3, state_hgrn, l, n_s, len_s, False)
        yf = _fourier_two_stage(zr, zi, n_s, len_s)
        xs = _merge(ya, yf, zb, pa, xs, pe_l, mod, wpa_bf, wpb_bf, wo_bf, fw, l, len_s, True)

    y_prompt = xp.reshape(n_p, len_p, D)
    y_sample = xs.reshape(n_s, len_s, D)
    return (y_prompt, y_sample, new_state)
```

```python
import functools

import jax
import jax.numpy as jnp
import numpy as np
from jax import lax
from jax.experimental import pallas as pl
from jax.experimental.pallas import tpu as pltpu

D = 1024
DEPTH = 4
N_HEADS = 8
HEAD = 128
D_A = N_HEADS * HEAD
N_GROUPS = 4
GROUP = 128
D_B = N_GROUPS * GROUP
D_IN = 5 * D_A + 2 * D_B + 2 * D
GRID_W = 64
EPS = 1e-6

V7X_VMEM_LIMIT_BYTES = 56 * 1024 * 1024

SCAN_CHUNK = 64
SCAN_GROUP = 32
SCAN_UNROLL = 4
SCAN_SHORT_SEQ = 512
SCAN_HEADS_SHORT = 8
EXP_CLAMP = 80.0
TM_PROJ = 256
CUM_ROWS = 256
TM_MERGE = 512
FFT_RADIX = 64
FFT_BLOCK = 32
MOD_ROWS = 8

F32 = jnp.float32
BF16 = jnp.bfloat16


def _cparams(n_axes):
    return pltpu.CompilerParams(
        dimension_semantics=("arbitrary",) * n_axes,
        vmem_limit_bytes=V7X_VMEM_LIMIT_BYTES,
    )


def _dot(a, b):
    return jnp.dot(a, b, preferred_element_type=F32)


def _dot_nt(a, b):
    return lax.dot_general(a, b, (((1,), (1,)), ((), ())), preferred_element_type=F32)


def _dot_tn(a, b):
    return lax.dot_general(a, b, (((0,), (0,)), ((), ())), preferred_element_type=F32)


def _sigmoid(x):
    return 1.0 / (1.0 + jnp.exp(-x))


def _silu(x):
    return x * _sigmoid(x)


def _dft_cos_sin(n):
    k = np.arange(n, dtype=np.float64)
    ang = 2.0 * np.pi * np.outer(k, k) / n
    return np.cos(ang), np.sin(ang)


def _pos_embed_table(length, d):
    rows = length // GRID_W
    r = np.repeat(np.arange(rows, dtype=np.float64), GRID_W)
    col = np.tile(np.arange(GRID_W, dtype=np.float64), rows)
    nf = d // 4
    freqs = 1.0 / (10000.0 ** (np.arange(nf, dtype=np.float64) / nf))

    def emb(p):
        a = p[:, None] * freqs[None, :]
        return np.concatenate([np.sin(a), np.cos(a)], axis=-1)

    return np.concatenate([emb(r), emb(col)], axis=-1).astype(np.float32)


def _lb_kernel(raw_ref, out_ref):
    x = raw_ref[...]
    m = jnp.max(x, axis=0, keepdims=True)
    e = jnp.exp(x - m)
    p = e / jnp.sum(e, axis=0, keepdims=True)
    cs = p[0:1]
    first = cs
    out_ref[0:1, :] = cs - first
    for l in range(1, DEPTH):
        cs = cs + p[l:l + 1]
        out_ref[l:l + 1, :] = cs - first


def _lower_bounds(lb_raw):
    raw = lb_raw.reshape(DEPTH, 2 * D_A)
    out = pl.pallas_call(
        _lb_kernel,
        out_shape=jax.ShapeDtypeStruct((DEPTH, 2 * D_A), F32),
        name="lower_bounds",
    )(raw)
    return out.reshape(DEPTH * 2, 1, D_A)


def _mod_kernel(cond_ref, w_ref, b_ref, out_ref):
    a = _silu(cond_ref[...])
    out_ref[...] = jnp.dot(a, w_ref[...], preferred_element_type=F32,
                           precision=lax.Precision.HIGHEST) + b_ref[...]


def _modulation(cond, w_ada, b_ada):
    tn = 512
    out = pl.pallas_call(
        _mod_kernel,
        grid=(DEPTH, 3 * D // tn),
        in_specs=[
            pl.BlockSpec((MOD_ROWS, D), lambda l, j: (0, 0)),
            pl.BlockSpec((None, D, tn), lambda l, j: (l, 0, j)),
            pl.BlockSpec((None, 1, tn), lambda l, j: (l, 0, j)),
        ],
        out_specs=pl.BlockSpec((None, MOD_ROWS, tn), lambda l, j: (l, 0, j)),
        out_shape=jax.ShapeDtypeStruct((DEPTH, MOD_ROWS, 3 * D), F32),
        compiler_params=_cparams(2),
        name="modulation",
    )(cond, w_ada, b_ada.reshape(DEPTH, 1, 3 * D))
    return out.reshape(DEPTH * MOD_ROWS, 1, 3 * D)


_Q0, _FF0, _V0, _ZA0, _U0, _ZB0, _GA0, _GB0 = (
    0, D_A, 3 * D_A, 4 * D_A, 5 * D_A, 5 * D_A + D_B, 5 * D_A + 2 * D_B, 5 * D_A + 2 * D_B + D)
PA_Q, PA_V, PA_Z, PA_GA, PA_GB, PA_KF = 0, 1, 2, 3, 4, 5
PA_WIDTH = 7 * D


def _forget_gate(x, lb):
    f = lb + (1.0 - lb) * _sigmoid(x)
    log_f = jnp.where(f < 1e-30, jnp.minimum(x, 0.0), jnp.log(jnp.maximum(f, 1e-30)))
    return log_f, 1.0 - f


def _proj_kernel(has_pe, *refs):
    refs = list(refs)
    x = refs[0][...]
    pos = 1
    if has_pe:
        x = x + refs[pos][...]
        pos += 1
    mod_ref, nw_ref, lbf_ref, lbb_ref, w_ref, cc_ref, sc_ref = refs[pos:pos + 7]
    pf_ref, pa_ref, zr_ref, zi_ref, zb_ref = refs[pos + 7:]

    ms = jnp.mean(x * x, axis=-1, keepdims=True)
    y = x * lax.rsqrt(ms + EPS) * nw_ref[...]
    shift = mod_ref[:, 0:D]
    scale = mod_ref[:, D:2 * D]
    h = (y * (1.0 + scale) + shift).astype(BF16)

    def sec(c0, width):
        return _dot(h, w_ref[:, c0:c0 + width])

    def put(block, val):
        pa_ref[:, block * D:(block + 1) * D] = val.astype(BF16)

    row = lax.broadcasted_iota(jnp.int32, (CUM_ROWS, CUM_ROWS), 0)
    col = lax.broadcasted_iota(jnp.int32, (CUM_ROWS, CUM_ROWS), 1)
    same_chunk = (row // SCAN_CHUNK) == (col // SCAN_CHUNK)
    tris = (jnp.where(same_chunk & (col <= row), 1.0, 0.0).astype(BF16),
            jnp.where(same_chunk & (col >= row), 1.0, 0.0).astype(BF16))
    for d, lb_ref in enumerate((lbf_ref, lbb_ref)):
        g, kk = _forget_gate(sec(_FF0 + d * D_A, D_A), lb_ref[...])
        g_hi = g.astype(BF16)
        g_lo = (g - g_hi.astype(F32)).astype(BF16)
        for r0 in range(0, x.shape[0], CUM_ROWS):
            rows = slice(r0, r0 + CUM_ROWS)
            pf_ref[rows, d * D_A:(d + 1) * D_A] = _dot(tris[d], g_hi[rows]) + _dot(tris[d], g_lo[rows])
        put(PA_KF + d, kk)

    put(PA_Q, _silu(sec(_Q0, D_A)))
    put(PA_V, sec(_V0, D_A))
    put(PA_Z, _silu(sec(_ZA0, D_A)))
    put(PA_GA, _sigmoid(sec(_GA0, D)))
    put(PA_GB, _sigmoid(sec(_GB0, D)))
    zb_ref[...] = _silu(sec(_ZB0, D_B)).astype(BF16)
    u = sec(_U0, D_B).astype(BF16)
    cc = cc_ref[...].astype(BF16)
    sc = sc_ref[...].astype(BF16)
    for g in range(N_GROUPS):
        ug = u[:, g * GROUP:(g + 1) * GROUP]
        zr_ref[:, g * GROUP:(g + 1) * GROUP] = _dot(ug, cc)
        zi_ref[:, g * GROUP:(g + 1) * GROUP] = -_dot(ug, sc)


def _proj(x, pe, mod, norm_w3, lbs, w_in_bf, cc, sc, layer, seq_len, latent):
    tokens = x.shape[0]
    tm = TM_PROJ
    has_pe = pe is not None
    per_seq = seq_len // tm
    if latent:
        mod_idx = lambda i: (layer * MOD_ROWS + 1 + i // per_seq, 0, 0)
    else:
        mod_idx = lambda i: (layer * MOD_ROWS, 0, 0)
    in_specs = [pl.BlockSpec((tm, D), lambda i: (i, 0))]
    args = [x]
    if has_pe:
        in_specs.append(pl.BlockSpec((tm, D), lambda i: (i % per_seq, 0)))
        args.append(pe)
    in_specs += [
        pl.BlockSpec((None, 1, 3 * D), mod_idx),
        pl.BlockSpec((None, 1, D), lambda i: (layer, 0, 0)),
        pl.BlockSpec((None, 1, D_A), lambda i: (2 * layer, 0, 0)),
        pl.BlockSpec((None, 1, D_A), lambda i: (2 * layer + 1, 0, 0)),
        pl.BlockSpec((None, D, D_IN), lambda i: (layer, 0, 0), pipeline_mode=pl.Buffered(1)),
        pl.BlockSpec((GROUP, GROUP), lambda i: (0, 0)),
        pl.BlockSpec((GROUP, GROUP), lambda i: (0, 0)),
    ]
    args += [mod, norm_w3, lbs, lbs, w_in_bf, cc, sc]
    out_shape = (
        jax.ShapeDtypeStruct((tokens, 2 * D_A), F32),
        jax.ShapeDtypeStruct((tokens, PA_WIDTH), BF16),
        jax.ShapeDtypeStruct((tokens, D_B), F32),
        jax.ShapeDtypeStruct((tokens, D_B), F32),
        jax.ShapeDtypeStruct((tokens, D_B), BF16),
    )
    out_specs = (
        pl.BlockSpec((tm, 2 * D_A), lambda i: (i, 0)),
        pl.BlockSpec((tm, PA_WIDTH), lambda i: (i, 0)),
        pl.BlockSpec((tm, D_B), lambda i: (i, 0)),
        pl.BlockSpec((tm, D_B), lambda i: (i, 0)),
        pl.BlockSpec((tm, D_B), lambda i: (i, 0)),
    )
    return pl.pallas_call(
        functools.partial(_proj_kernel, has_pe),
        grid=(tokens // tm,),
        in_specs=in_specs,
        out_specs=out_specs,
        out_shape=out_shape,
        compiler_params=_cparams(1),
        name="proj",
    )(*args)


def _scan_kernel(seq_len, heads, has_init, emit_state, has_acc, *refs):
    refs = list(refs)
    bf_ref, bb_ref, q_ref, v_ref, z_ref, kf_ref, kb_ref, gn_ref = refs[:8]
    pos = 8
    init_ref = None
    if has_init:
        init_ref = refs[pos]
        pos += 1
    if has_acc:
        pos += 1
    y_ref = refs[pos]
    pos += 1
    st_ref = None
    if emit_state:
        st_ref = refs[pos]
        pos += 1
    (o_ref, sf_ref, sb_ref, qinf_ref, qinb_ref, uf_ref, ub_ref, elf_ref, elb_ref) = refs[pos:pos + 9]

    C = SCAN_CHUNK
    H = C // 2
    n_chunks = seq_len // C
    G = min(SCAN_GROUP, n_chunks)
    GC = G * C
    n_groups = seq_len // GC
    lanes = [slice(hd * HEAD, (hd + 1) * HEAD) for hd in range(heads)]

    def pair_masks(n):
        rr = lax.broadcasted_iota(jnp.int32, (n, n), 0)
        cc = lax.broadcasted_iota(jnp.int32, (n, n), 1)
        return cc <= rr, cc >= rr

    mask_c = pair_masks(C)
    mask_h = pair_masks(H)
    dirs = (
        (bf_ref, kf_ref, C - 1, qinf_ref, uf_ref, elf_ref),
        (bb_ref, kb_ref, 0, qinb_ref, ub_ref, elb_ref),
    )

    def split_factors(bq, qv, kv):
        n = bq.shape[0]
        r = 0.5 * (bq[0:1, :] + bq[n - 1:n, :])
        e = jnp.exp(jnp.clip(bq - r, -EXP_CLAMP, EXP_CLAMP))
        return qv * e, kv * (1.0 / e), r, jnp.abs(bq[0:1, :] - r)

    def halves(d, bc, qc, kc, vc):
        first, second = (slice(0, H), slice(H, C)) if d == 0 else (slice(H, C), slice(0, H))
        edge = bc[H - 1:H, :] if d == 0 else bc[H:H + 1, :]
        atts, dev = [], None
        for rows in (first, second):
            qt, kt, _, dev_h = split_factors(bc[rows], qc[rows], kc[rows])
            att = _dot_nt(qt.astype(BF16), kt.astype(BF16))
            atts.append(jnp.where(mask_h[d], att, 0.0).astype(BF16))
            dev = dev_h if dev is None else jnp.maximum(dev, dev_h)
        cross = _dot_nt((qc[second] * jnp.exp(bc[second] - edge)).astype(BF16),
                        (kc[first] * jnp.exp(edge - bc[first])).astype(BF16)).astype(BF16)
        o_first = _dot(atts[0], vc[first])
        o_second = _dot(atts[1], vc[second]) + _dot(cross, vc[first])
        parts = [o_first, o_second] if d == 0 else [o_second, o_first]
        return jnp.concatenate(parts, axis=0), dev

    def pairwise(d, bc, qc, kc, vf):
        s_idx = lax.broadcasted_iota(jnp.int32, (C, 1), 0)

        def row(t, o_blk):
            sel = s_idx == t
            b_t = jnp.sum(jnp.where(sel, bc, 0.0), axis=0, keepdims=True)
            q_t = jnp.sum(jnp.where(sel, qc, 0.0), axis=0, keepdims=True)
            allowed = (s_idx <= t) if d == 0 else (s_idx >= t)
            decay = jnp.where(allowed, jnp.exp(jnp.where(allowed, b_t - bc, 0.0)), 0.0)
            att_t = jnp.sum(decay * (q_t * kc), axis=1, keepdims=True)
            o_t = jnp.sum(att_t * vf, axis=0, keepdims=True)
            return jnp.where(sel, o_t, o_blk)

        return lax.fori_loop(0, C, row, jnp.zeros((C, HEAD), F32))

    def redo_chunk(hd, chunk, rows):
        ls = lanes[hd]
        bcs = [b_ref[rows, ls] for b_ref, _, _, _, _, _ in dirs]
        span = None
        for bc in bcs:
            r = 0.5 * (bc[0:1, :] + bc[C - 1:C, :])
            dev = jnp.maximum(jnp.abs(bc[0:1, :] - r), jnp.abs(r))
            span = dev if span is None else jnp.maximum(span, dev)

        @pl.when(jnp.max(span) > EXP_CLAMP)
        def _():
            qc = q_ref[rows, ls].astype(F32)
            vc = v_ref[rows, ls]
            kcs = [k_ref[rows, ls].astype(F32) for _, k_ref, _, _, _, _ in dirs]
            o_sum, span_h = None, None
            for d, (_, _, last_row, qin_ref, u_ref, _) in enumerate(dirs):
                bc, kc = bcs[d], kcs[d]
                b_last = bc[last_row:last_row + 1, :]
                qin_ref[rows, ls] = (qc * jnp.exp(bc)).astype(BF16)
                u_ref[hd * n_chunks + chunk] = _dot_tn(vc, (kc * jnp.exp(b_last - bc)).astype(BF16))
                o_d, dev = halves(d, bc, qc, kc, vc)
                o_sum = o_d if o_sum is None else o_sum + o_d
                span_h = dev if span_h is None else jnp.maximum(span_h, dev)
            o_ref[rows, ls] = o_sum

            @pl.when(jnp.max(span_h) > EXP_CLAMP)
            def _():
                vf = vc.astype(F32)
                o_ref[rows, ls] = pairwise(0, bcs[0], qc, kcs[0], vf) + pairwise(1, bcs[1], qc, kcs[1], vf)

    def local(j, carry):
        r0 = pl.multiple_of(j * GC, GC)
        work = []
        span = None
        for hd, ls in enumerate(lanes):
            q = q_ref[pl.ds(r0, GC), ls].astype(F32)
            v = v_ref[pl.ds(r0, GC), ls]
            for d, (b_ref, k_ref, last_row, qin_ref, u_ref, el_ref) in enumerate(dirs):
                b = b_ref[pl.ds(r0, GC), ls]
                kk = k_ref[pl.ds(r0, GC), ls].astype(F32)
                for c in range(G):
                    sl = slice(c * C, (c + 1) * C)
                    bc = b[sl]
                    b_last = bc[last_row:last_row + 1, :]
                    qt, kt, r, dev = split_factors(bc, q[sl], kk[sl])
                    dev = jnp.maximum(dev, jnp.abs(r))
                    span = dev if span is None else jnp.maximum(span, dev)
                    qin_ref[pl.ds(r0 + c * C, C), ls] = (qt * jnp.exp(r)).astype(BF16)
                    el_ref[hd * n_chunks + j * G + c] = jnp.exp(b_last)
                    k_out = (kt * jnp.exp(b_last - r)).astype(BF16)
                    work.append((qt.astype(BF16), kt.astype(BF16), k_out, mask_c[d], v[sl], u_ref,
                                 hd * n_chunks + j * G + c))
        atts = [_dot_nt(qt, kt) for qt, kt, _, _, _, _, _ in work]
        atts = [jnp.where(w[3], a, 0.0).astype(BF16) for w, a in zip(work, atts)]
        for (_, _, k_out, _, vc, u_ref, slot) in work:
            u_ref[slot] = _dot_tn(vc, k_out)
        outs = [_dot(a, w[4]) for w, a in zip(work, atts)]
        for hd, ls in enumerate(lanes):
            o_f = jnp.concatenate(outs[(2 * hd) * G:(2 * hd + 1) * G], axis=0)
            o_b = jnp.concatenate(outs[(2 * hd + 1) * G:(2 * hd + 2) * G], axis=0)
            o_ref[pl.ds(r0, GC), ls] = o_f + o_b

        @pl.when(jnp.max(span) > EXP_CLAMP)
        def _():
            for hd in range(heads):
                def redo(c, carry, hd=hd):
                    redo_chunk(hd, j * G + c, pl.ds(pl.multiple_of(j * GC + c * C, C), C))
                    return carry

                lax.fori_loop(0, G, redo, 0)

        return carry

    lax.fori_loop(0, n_groups, local, 0)

    for hd in range(heads):
        base = hd * n_chunks
        if has_init:
            s_init = (init_ref[0, hd].T, init_ref[1, hd].T)
        else:
            s_init = (jnp.zeros((HEAD, HEAD), F32), jnp.zeros((HEAD, HEAD), F32))

        def step(i, carry, base=base):
            s_f, s_b = carry
            kf = base + i
            kb = base + n_chunks - 1 - i
            sf_ref[kf] = s_f.astype(BF16)
            sb_ref[kb] = s_b.astype(BF16)
            return (s_f * elf_ref[kf] + uf_ref[kf], s_b * elb_ref[kb] + ub_ref[kb])

        s_f, s_b = lax.fori_loop(0, n_chunks, step, s_init, unroll=SCAN_UNROLL)
        if emit_state:
            st_ref[0, hd] = s_f.T
            st_ref[1, hd] = s_b.T

    def add_state_readout(j):
        r0 = pl.multiple_of(j * GC, GC)
        for hd, ls in enumerate(lanes):
            o = o_ref[pl.ds(r0, GC), ls]
            for qin_ref, s_ref in ((qinf_ref, sf_ref), (qinb_ref, sb_ref)):
                o = o + jnp.concatenate(
                    [_dot_nt(qin_ref[pl.ds(r0 + c * C, C), ls], s_ref[hd * n_chunks + j * G + c])
                     for c in range(G)], axis=0)
            o_ref[pl.ds(r0, GC), ls] = o

    def normalise(j):
        r0 = pl.multiple_of(j * GC, GC)
        for ls in lanes:
            o = o_ref[pl.ds(r0, GC), ls]
            ms = jnp.mean(o * o, axis=-1, keepdims=True)
            o = o * lax.rsqrt(ms + EPS) * gn_ref[:, ls]
            y_ref[pl.ds(r0, GC), ls] = (o * z_ref[pl.ds(r0, GC), ls].astype(F32)).astype(BF16)

    add_state_readout(0)

    def finish(j, carry):
        normalise(j - 1)
        add_state_readout(j)
        return carry

    lax.fori_loop(1, n_groups, finish, 0)
    normalise(n_groups - 1)


def _scan(pf, pa, gnorm3, init_state, layer, n_seq, seq_len, emit_state, state_acc=None):
    tokens = n_seq * seq_len
    n_chunks = seq_len // SCAN_CHUNK
    has_init = init_state is not None
    has_acc = state_acc is not None
    heads = SCAN_HEADS_SHORT if seq_len <= SCAN_SHORT_SEQ else 1
    width = heads * HEAD
    blk = lambda c0: pl.BlockSpec((seq_len, width), lambda b, h, c0=c0: (b, c0 // heads + h))
    in_specs = [
        blk(0), blk(N_HEADS),
        blk(PA_Q * N_HEADS), blk(PA_V * N_HEADS), blk(PA_Z * N_HEADS),
        blk(PA_KF * N_HEADS), blk((PA_KF + 1) * N_HEADS),
        pl.BlockSpec((None, 1, width), lambda b, h: (layer, 0, h)),
    ]
    args = [pf, pf, pa, pa, pa, pa, pa, gnorm3]
    if has_init:
        in_specs.append(pl.BlockSpec((None, None, 2, heads, HEAD, HEAD),
                                     lambda b, h: (b, layer, 0, h, 0, 0)))
        args.append(init_state)
    aliases = {}
    if has_acc:
        aliases = {len(args): 1}
        in_specs.append(pl.BlockSpec(memory_space=pl.ANY))
        args.append(state_acc)
    out_shape = [jax.ShapeDtypeStruct((tokens, D_A), BF16)]
    out_specs = [pl.BlockSpec((seq_len, width), lambda b, h: (b, h))]
    if emit_state:
        out_shape.append(jax.ShapeDtypeStruct((n_seq, DEPTH, 2, N_HEADS, HEAD, HEAD), F32))
        out_specs.append(pl.BlockSpec((None, None, 2, heads, HEAD, HEAD),
                                      lambda b, h: (b, layer, 0, h, 0, 0)))
    slots = heads * n_chunks
    res = pl.pallas_call(
        functools.partial(_scan_kernel, seq_len, heads, has_init, emit_state, has_acc),
        grid=(n_seq, N_HEADS // heads),
        in_specs=in_specs,
        out_specs=out_specs,
        out_shape=out_shape,
        input_output_aliases=aliases,
        scratch_shapes=[
            pltpu.VMEM((seq_len, width), F32),
            pltpu.VMEM((slots, HEAD, HEAD), BF16), pltpu.VMEM((slots, HEAD, HEAD), BF16),
            pltpu.VMEM((seq_len, width), BF16), pltpu.VMEM((seq_len, width), BF16),
            pltpu.VMEM((slots, HEAD, HEAD), F32), pltpu.VMEM((slots, HEAD, HEAD), F32),
            pltpu.VMEM((slots, 1, HEAD), F32), pltpu.VMEM((slots, 1, HEAD), F32),
        ],
        compiler_params=_cparams(2),
        name="scan",
    )(*args)
    return res if emit_state else (res[0], None)


def _fourier_stage1_kernel(zr_ref, zi_ref, m_ref, tc_ref, ts_ref, br_ref, bi_ref):
    m = m_ref[...].astype(BF16)
    for n in range(FFT_BLOCK):
        z = jnp.concatenate([zr_ref[:, n, :], zi_ref[:, n, :]], axis=0).astype(BF16)
        a = _dot(m, z)
        ar = a[:FFT_RADIX]
        ai = a[FFT_RADIX:]
        tc = jnp.concatenate([tc_ref[n]] * N_GROUPS, axis=1)
        ts = jnp.concatenate([ts_ref[n]] * N_GROUPS, axis=1)
        br_ref[:, n, :] = ar * tc + ai * ts
        bi_ref[:, n, :] = ai * tc - ar * ts


def _fourier_stage2_kernel(scale, br_ref, bi_ref, cs_ref, y_ref):
    cs = cs_ref[...].astype(BF16)
    for j in range(FFT_BLOCK):
        bcat = jnp.concatenate([br_ref[j], bi_ref[j]], axis=0).astype(BF16)
        y_ref[:, j, :] = _dot(cs, bcat) * scale


def _fourier_two_stage(zr, zi, n_seq, seq_len):
    R = FFT_RADIX
    assert seq_len == R * R
    c, s = _dft_cos_sin(R)
    m1 = np.block([[c, s], [-s, c]])
    idx = np.arange(R, dtype=np.float64)
    ang = 2.0 * np.pi * np.outer(idx, idx) / seq_len
    tc = np.repeat(np.cos(ang)[:, :, None], GROUP, axis=2).astype(np.float32)
    ts = np.repeat(np.sin(ang)[:, :, None], GROUP, axis=2).astype(np.float32)
    cs = np.concatenate([c, s], axis=1)
    scale = 1.0 / np.sqrt(seq_len * GROUP)

    fast_blk = pl.BlockSpec((None, R, FFT_BLOCK, D_B), lambda b, j: (b, 0, j, 0))
    slow_blk = pl.BlockSpec((None, FFT_BLOCK, R, D_B), lambda b, j: (b, j, 0, 0))
    tw_blk = pl.BlockSpec((FFT_BLOCK, R, GROUP), lambda b, j: (j, 0, 0))
    shape4 = jax.ShapeDtypeStruct((n_seq, R, R, D_B), F32)
    br, bi = pl.pallas_call(
        _fourier_stage1_kernel,
        grid=(n_seq, R // FFT_BLOCK),
        in_specs=[fast_blk, fast_blk, pl.BlockSpec((2 * R, 2 * R), lambda b, j: (0, 0)), tw_blk, tw_blk],
        out_specs=(fast_blk, fast_blk),
        out_shape=(shape4, shape4),
        compiler_params=_cparams(2),
        name="fourier_stage1",
    )(zr.reshape(n_seq, R, R, D_B), zi.reshape(n_seq, R, R, D_B),
      jnp.asarray(m1, F32), jnp.asarray(tc), jnp.asarray(ts))

    y = pl.pallas_call(
        functools.partial(_fourier_stage2_kernel, scale),
        grid=(n_seq, R // FFT_BLOCK),
        in_specs=[slow_blk, slow_blk, pl.BlockSpec((R, 2 * R), lambda b, j: (0, 0))],
        out_specs=fast_blk,
        out_shape=shape4,
        compiler_params=_cparams(2),
        name="fourier_stage2",
    )(br, bi, jnp.asarray(cs, F32))
    return y.reshape(n_seq * seq_len, D_B)


def _merge_kernel(has_pe, final, dense_len, *refs):
    refs = list(refs)
    if dense_len:
        ya_ref, zr_ref, zi_ref, c_ref, s_ref, zb_ref, ga_ref, gb_ref, x_ref = refs[:9]
        pos = 9
        c = c_ref[...].astype(BF16)
        s = s_ref[...].astype(BF16)
        parts = []
        for r0 in range(0, x_ref.shape[0], dense_len):
            rows = slice(r0, r0 + dense_len)
            parts.append(_dot(c, zr_ref[rows, :].astype(BF16)) + _dot(s, zi_ref[rows, :].astype(BF16)))
        yf = jnp.concatenate(parts, axis=0) * (1.0 / np.sqrt(dense_len * GROUP))
    else:
        ya_ref, yf_ref, zb_ref, ga_ref, gb_ref, x_ref = refs[:6]
        pos = 6
        yf = yf_ref[...]
    x = x_ref[...]
    if has_pe:
        x = x + refs[pos][...]
        pos += 1
    mod_ref, wpa_ref, wpb_ref, wo_ref = refs[pos:pos + 4]
    pos += 4
    if final:
        fw_ref = refs[pos]
        pos += 1
    out_ref = refs[pos]
    yb = (yf * zb_ref[...].astype(F32)).astype(BF16)
    merged = (ga_ref[...].astype(F32) * _dot(ya_ref[...], wpa_ref[...])
              + gb_ref[...].astype(F32) * _dot(yb, wpb_ref[...]))
    out = _dot(merged.astype(BF16), wo_ref[...])
    xn = x + mod_ref[:, 2 * D:3 * D] * out
    if final:
        ms = jnp.mean(xn * xn, axis=-1, keepdims=True)
        xn = xn * lax.rsqrt(ms + EPS) * fw_ref[...]
    out_ref[...] = xn


def _merge(ya, yf, zb, pa, x, pe, mod, wpa, wpb, wo, final_w, layer, seq_len, latent):
    tokens = x.shape[0]
    tm = TM_MERGE
    has_pe = pe is not None
    final = final_w is not None
    dense_len = seq_len if isinstance(yf, tuple) else 0
    row = lambda i: (i, 0)
    if latent:
        mod_idx = lambda i: (layer * MOD_ROWS + 1 + (i * tm) // seq_len, 0, 0)
    else:
        mod_idx = lambda i: (layer * MOD_ROWS, 0, 0)
    in_specs = [pl.BlockSpec((tm, D_A), row)]
    args = [ya]
    if dense_len:
        c, s = _dft_cos_sin(seq_len)
        mat = pl.BlockSpec((seq_len, seq_len), lambda i: (0, 0))
        in_specs += [pl.BlockSpec((tm, D_B), row), pl.BlockSpec((tm, D_B), row), mat, mat]
        args += [yf[0], yf[1], jnp.asarray(c, F32), jnp.asarray(s, F32)]
    else:
        in_specs.append(pl.BlockSpec((tm, D_B), row))
        args.append(yf)
    in_specs += [
        pl.BlockSpec((tm, D_B), row),
        pl.BlockSpec((tm, D), lambda i: (i, PA_GA)),
        pl.BlockSpec((tm, D), lambda i: (i, PA_GB)),
        pl.BlockSpec((tm, D), row),
    ]
    args += [zb, pa, pa, x]
    if has_pe:
        per_seq = seq_len // tm
        in_specs.append(pl.BlockSpec((tm, D), lambda i: (i % per_seq, 0)))
        args.append(pe)
    in_specs += [
        pl.BlockSpec((None, 1, 3 * D), mod_idx),
        pl.BlockSpec((None, D_A, D), lambda i: (layer, 0, 0)),
        pl.BlockSpec((None, D_B, D), lambda i: (layer, 0, 0)),
        pl.BlockSpec((None, D, D), lambda i: (layer, 0, 0)),
    ]
    args += [mod, wpa, wpb, wo]
    if final:
        in_specs.append(pl.BlockSpec((1, D), lambda i: (0, 0)))
        args.append(final_w)
    return pl.pallas_call(
        functools.partial(_merge_kernel, has_pe, final, dense_len),
        grid=(tokens // tm,),
        in_specs=in_specs,
        out_specs=pl.BlockSpec((tm, D), row),
        out_shape=jax.ShapeDtypeStruct((tokens, D), F32),
        compiler_params=_cparams(1),
        name="merge",
    )(*args)


def kernel(x_prompt, x_sample, state_hgrn, c, c_ctx, norm_w, w_ada, b_ada, w_in, lb_raw,
           gnorm_w, w_pa, w_pb, w_o, final_norm_w):
    n_p, len_p, _ = x_prompt.shape
    n_s, len_s, _ = x_sample.shape

    cond = jnp.zeros((MOD_ROWS, D), F32).at[0].set(c_ctx).at[1:1 + n_s].set(c)
    mod = _modulation(cond, w_ada, b_ada)
    lbs = _lower_bounds(lb_raw)

    w_in_bf = w_in.astype(BF16)
    wpa_bf = w_pa.astype(BF16)
    wpb_bf = w_pb.astype(BF16)
    wo_bf = w_o.astype(BF16)
    norm_w3 = norm_w.reshape(DEPTH, 1, D)
    gnorm3 = gnorm_w.reshape(DEPTH, 1, D_A)
    final_w = final_norm_w.reshape(1, D)

    cc_np, sc_np = _dft_cos_sin(GROUP)
    cc = jnp.asarray(cc_np, F32)
    sc = jnp.asarray(sc_np, F32)
    pe = jnp.asarray(_pos_embed_table(len_s, D))

    xp = x_prompt.reshape(n_p * len_p, D)
    xs = x_sample.reshape(n_s * len_s, D)
    new_state = None
    for l in range(DEPTH):
        last = l == DEPTH - 1
        fw = final_w if last else None
        pe_l = pe if l == 0 else None

        pf, pa, zr, zi, zb = _proj(xp, None, mod, norm_w3, lbs, w_in_bf, cc, sc, l, len_p, False)
        ya, new_state = _scan(pf, pa, gnorm3, None, l, n_p, len_p, True, new_state)
        xp = _merge(ya, (zr, zi), zb, pa, xp, None, mod, wpa_bf, wpb_bf, wo_bf, fw, l, len_p, False)

        pf, pa, zr, zi, zb = _proj(xs, pe_l, mod, norm_w3, lbs, w_in_bf, cc, sc, l, len_s, True)
        ya, _ = _scan(pf, pa, gnorm3, state_hgrn, l, n_s, len_s, False)
        yf = _fourier_two_stage(zr, zi, n_s, len_s)
        xs = _merge(ya, yf, zb, pa, xs, pe_l, mod, wpa_bf, wpb_bf, wo_bf, fw, l, len_s, True)

    y_prompt = xp.reshape(n_p, len_p, D)
    y_sample = xs.reshape(n_s, len_s, D)
    return (y_prompt, y_sample, new_state)
```
